```python
import jax, jax.numpy as jnp
from jax import lax
import numpy as np

D_MODEL = 1024
BATCH = 4
SEQ = 4096
DEPTH = 4

CHUNK = 64
N_MIXERS = 3
CONV_A_WIDTH = 31
CONV_C_WIDTH = 3
SGU_BLOCK = 128
SGU_HEADS = 8
SGU_WIDTH = 2 * D_MODEL
SGU_GROUP = SGU_WIDTH // SGU_HEADS
D_FF = -(-8 * D_MODEL // (3 * 256)) * 256
D_PLE = 256
ALPHA = (2 * DEPTH) ** 0.25
BETA = (8 * DEPTH) ** -0.25
LN_EPS = 1e-5
N_A = (DEPTH + 2) // 3
N_B = (DEPTH + 1) // 3
N_C = DEPTH // 3

kernel_name = "interleaved_conv_sgu_shortconv_deepnorm_trunk"


def layer_norm(x, g, b):
    xf = x.astype(jnp.float32)
    mu = jnp.mean(xf, axis=-1, keepdims=True)
    xc = xf - mu
    var = jnp.mean(xc * xc, axis=-1, keepdims=True)
    y = xc * lax.rsqrt(var + LN_EPS) * g.astype(jnp.float32) + b.astype(jnp.float32)
    return y.astype(x.dtype)


def rms_norm(x, g):
    xf = x.astype(jnp.float32)
    y = xf * lax.rsqrt(jnp.mean(xf * xf, axis=-1, keepdims=True) + LN_EPS) * g.astype(jnp.float32)
    return y.astype(x.dtype)


def causal_depthwise_conv(x, w):
    k, c = w.shape
    return lax.conv_general_dilated(
        x, w[:, None, :].astype(x.dtype), window_strides=(1,), padding=[(k - 1, 0)],
        dimension_numbers=("NWC", "WIO", "NWC"), feature_group_count=c)


def mixer_conformer_conv(x, w_pw1, b_pw1, w_dw, b_dw, ln_g, ln_b, w_pw2):
    h = x @ w_pw1 + b_pw1
    a, g = jnp.split(h, 2, axis=-1)
    h = a * jax.nn.sigmoid(g)
    h = causal_depthwise_conv(h, w_dw) + b_dw
    h = jax.nn.silu(layer_norm(h, ln_g, ln_b))
    return h @ w_pw2


def mixer_sgu(x, w_in, b_in, ln_g, ln_b, w_s, b_s, w_out):
    bsz, seq, _ = x.shape
    z = jax.nn.gelu(x @ w_in + b_in)
    u, v = jnp.split(z, 2, axis=-1)
    v = layer_norm(v, ln_g, ln_b)
    v = v.reshape(bsz, seq // SGU_BLOCK, SGU_BLOCK, SGU_HEADS, SGU_GROUP)
    pos = jnp.arange(SGU_BLOCK) // CHUNK
    mask = pos[:, None] >= pos[None, :]
    w_m = jnp.where(mask[None], w_s, jnp.zeros_like(w_s))
    f = jnp.einsum("hts,bnshg->bnthg", w_m, v) + jnp.transpose(b_s)[None, None, :, :, None]
    return (u * f.reshape(bsz, seq, SGU_WIDTH)) @ w_out


def mixer_short_conv(x, w_in, w_conv, w_out):
    bg, cg, h = jnp.split(x @ w_in, 3, axis=-1)
    y = causal_depthwise_conv(cg * h, w_conv)
    return (bg * y) @ w_out


def swiglu(x, w_gate, w_up, w_down):
    return (jax.nn.silu(x @ w_gate) * (x @ w_up)) @ w_down


def setup_inputs(seed: int = 0) -> dict:
    key = jax.random.key(seed)
    ks = jax.random.split(key, 32)
    f32 = jnp.float32

    def nrm(k, shape, scale):
        return jax.random.normal(k, shape, f32) * scale

    D, E = D_MODEL, SGU_WIDTH
    return {
        "x": nrm(ks[0], (BATCH, SEQ, D), 1.0),
        "p": nrm(ks[1], (DEPTH, BATCH, SEQ, D_PLE), 1.0),
        "a_w_pw1": nrm(ks[2], (N_A, D, 2 * D), D ** -0.5),
        "a_b_pw1": nrm(ks[3], (N_A, 2 * D), 0.02),
        "a_w_dw": nrm(ks[4], (N_A, CONV_A_WIDTH, D), CONV_A_WIDTH ** -0.5),
        "a_b_dw": nrm(ks[5], (N_A, D), 0.02),
        "a_ln_g": 1.0 + nrm(ks[6], (N_A, D), 0.02),
        "a_ln_b": nrm(ks[7], (N_A, D), 0.02),
        "a_w_pw2": nrm(ks[8], (N_A, D, D), BETA * D ** -0.5),
        "b_w_in": nrm(ks[9], (N_B, D, 2 * E), D ** -0.5),
        "b_b_in": nrm(ks[10], (N_B, 2 * E), 0.02),
        "b_ln_g": 1.0 + nrm(ks[11], (N_B, E), 0.02),
        "b_ln_b": nrm(ks[12], (N_B, E), 0.02),
        "b_w_s": nrm(ks[13], (N_B, SGU_HEADS, SGU_BLOCK, SGU_BLOCK), SGU_BLOCK ** -0.5),
        "b_b_s": 1.0 + nrm(ks[14], (N_B, SGU_HEADS, SGU_BLOCK), 0.02),
        "b_w_out": nrm(ks[15], (N_B, E, D), BETA * E ** -0.5),
        "c_w_in": nrm(ks[16], (N_C, D, 3 * D), D ** -0.5),
        "c_w_conv": nrm(ks[17], (N_C, CONV_C_WIDTH, D), CONV_C_WIDTH ** -0.5),
        "c_w_out": nrm(ks[18], (N_C, D, D), BETA * D ** -0.5),
        "ln1_g": 1.0 + nrm(ks[19], (DEPTH, D), 0.02),
        "ln1_b": nrm(ks[20], (DEPTH, D), 0.02),
        "ln2_g": 1.0 + nrm(ks[21], (DEPTH, D), 0.02),
        "ln2_b": nrm(ks[22], (DEPTH, D), 0.02),
        "ffn_w_gate": nrm(ks[23], (DEPTH, D, D_FF), D ** -0.5),
        "ffn_w_up": nrm(ks[24], (DEPTH, D, D_FF), D ** -0.5),
        "ffn_w_down": nrm(ks[25], (DEPTH, D_FF, D), BETA * D_FF ** -0.5),
        "ple_w_gate": nrm(ks[26], (DEPTH, D, D), D ** -0.5),
        "ple_w_proj": nrm(ks[27], (DEPTH, D_PLE, D), D_PLE ** -0.5),
        "ple_norm_g": 1.0 + nrm(ks[28], (DEPTH, D), 0.02),
    }


def reference(x, p,
              a_w_pw1, a_b_pw1, a_w_dw, a_b_dw, a_ln_g, a_ln_b, a_w_pw2,
              b_w_in, b_b_in, b_ln_g, b_ln_b, b_w_s, b_b_s, b_w_out,
              c_w_in, c_w_conv, c_w_out,
              ln1_g, ln1_b, ln2_g, ln2_b,
              ffn_w_gate, ffn_w_up, ffn_w_down,
              ple_w_gate, ple_w_proj, ple_norm_g):
    for i in range(DEPTH):
        m, j = i % N_MIXERS, i // N_MIXERS
        if m == 0:
            h = mixer_conformer_conv(x, a_w_pw1[j], a_b_pw1[j], a_w_dw[j], a_b_dw[j],
                                     a_ln_g[j], a_ln_b[j], a_w_pw2[j])
        elif m == 1:
            h = mixer_sgu(x, b_w_in[j], b_b_in[j], b_ln_g[j], b_ln_b[j],
                          b_w_s[j], b_b_s[j], b_w_out[j])
        else:
            h = mixer_short_conv(x, c_w_in[j], c_w_conv[j], c_w_out[j])
        x = layer_norm(ALPHA * x + h, ln1_g[i], ln1_b[i])
        x = layer_norm(ALPHA * x + swiglu(x, ffn_w_gate[i], ffn_w_up[i], ffn_w_down[i]),
                       ln2_g[i], ln2_b[i])
        gate = jax.nn.sigmoid(x @ ple_w_gate[i])
        x = x + gate * rms_norm(p[i] @ ple_w_proj[i], ple_norm_g[i])
    return x
```

```python
import functools

import jax
import jax.numpy as jnp
from jax import lax
from jax.experimental import pallas as pl
from jax.experimental.pallas import tpu as pltpu

D_MODEL = 1024
BATCH = 4
SEQ = 4096
DEPTH = 4
CHUNK = 64
N_MIXERS = 3
CONV_A_WIDTH = 31
CONV_C_WIDTH = 3
SGU_BLOCK = 128
SGU_HEADS = 8
SGU_WIDTH = 2 * D_MODEL
SGU_GROUP = SGU_WIDTH // SGU_HEADS
D_FF = -(-8 * D_MODEL // (3 * 256)) * 256
D_PLE = 256
ALPHA = (2 * DEPTH) ** 0.25
LN_EPS = 1e-5

LANES = 128
SUBLANES = 8
N_LANE_BLOCKS = D_MODEL // LANES
TILE_M = 512
TILES_PER_SEQ = SEQ // TILE_M
CONV_ROWS = 32
VMEM_LIMIT_BYTES = 56 * 1024 * 1024

BF16 = jnp.bfloat16
F32 = jnp.float32


def _layer_norm(x, g, b):
    mu = jnp.mean(x, axis=-1, keepdims=True)
    xc = x - mu
    var = jnp.mean(xc * xc, axis=-1, keepdims=True)
    return xc * lax.rsqrt(var + LN_EPS) * g + b


def _dot(a, w):
    return jnp.dot(a, w, preferred_element_type=F32)


def _causal_conv(vals, w_ref, hist_ref, out_ref, *, width, pad):
    first = pl.program_id(0) % TILES_PER_SEQ == 0

    @pl.when(first)
    def _():
        hist_ref[:, 0:pad, :] = jnp.zeros((N_LANE_BLOCKS, pad, LANES), F32)

    @pl.when(jnp.logical_not(first))
    def _():
        hist_ref[:, 0:pad, :] = hist_ref[:, TILE_M:TILE_M + pad, :]

    for c in range(N_LANE_BLOCKS):
        hist_ref[c, pad:pad + TILE_M, :] = vals[:, c * LANES:(c + 1) * LANES]

    base = pad - (width - 1)

    def chunk(r, carry):
        r0 = pl.multiple_of(r * CONV_ROWS, CONV_ROWS)
        for c in range(N_LANE_BLOCKS):
            acc = None
            for k in range(width):
                win = hist_ref[c, pl.ds(r0 + base + k, CONV_ROWS), :]
                term = win * w_ref[k:k + 1, c * LANES:(c + 1) * LANES]
                acc = term if acc is None else acc + term
            out_ref[pl.ds(r0, CONV_ROWS), c * LANES:(c + 1) * LANES] = acc
        return carry

    lax.fori_loop(0, TILE_M // CONV_ROWS, chunk, 0)


def _mixer_a_kernel(x_ref, w1_ref, b1_ref, wdw_ref, bdw_ref, lng_ref, lnb_ref,
                    w2_ref, g1_ref, be1_ref, o_ref, hist_ref, conv_ref):
    x = x_ref[...]
    h = _dot(x.astype(BF16), w1_ref[...]) + b1_ref[...]
    glu = h[:, :D_MODEL] * jax.nn.sigmoid(h[:, D_MODEL:])
    _causal_conv(glu, wdw_ref, hist_ref, conv_ref, width=CONV_A_WIDTH, pad=32)
    y = conv_ref[...] + bdw_ref[...]
    y = jax.nn.silu(_layer_norm(y, lng_ref[...], lnb_ref[...]))
    m = _dot(y.astype(BF16), w2_ref[...])
    o_ref[...] = _layer_norm(ALPHA * x + m, g1_ref[...], be1_ref[...])


def _mixer_b_kernel(x_ref, win_ref, bin_ref, lng_ref, lnb_ref, ws_ref, bs_ref,
                    wout_ref, g1_ref, be1_ref, o_ref, gated_ref):
    x = x_ref[...]
    z = jax.nn.gelu(_dot(x.astype(BF16), win_ref[...]) + bin_ref[...])
    u = z[:, :SGU_WIDTH]
    v = _layer_norm(z[:, SGU_WIDTH:], lng_ref[...], lnb_ref[...]).astype(BF16)
    row = lax.broadcasted_iota(jnp.int32, (SGU_BLOCK, SGU_BLOCK), 0) // CHUNK
    col = lax.broadcasted_iota(jnp.int32, (SGU_BLOCK, SGU_BLOCK), 1) // CHUNK
    mask = row >= col
    for hd in range(SGU_HEADS):
        w_m = jnp.where(mask, ws_ref[hd], 0.0).astype(BF16)
        cs = slice(hd * SGU_GROUP, (hd + 1) * SGU_GROUP)
        for n in range(TILE_M // SGU_BLOCK):
            rs = slice(n * SGU_BLOCK, (n + 1) * SGU_BLOCK)
            f = _dot(w_m, v[rs, cs]) + bs_ref[:, cs]
            gated_ref[rs, cs] = (u[rs, cs] * f).astype(BF16)
    m = _dot(gated_ref[...], wout_ref[...])
    o_ref[...] = _layer_norm(ALPHA * x + m, g1_ref[...], be1_ref[...])


def _mixer_c_kernel(x_ref, win_ref, wconv_ref, wout_ref, g1_ref, be1_ref,
                    o_ref, hist_ref, conv_ref):
    x = x_ref[...]
    y = _dot(x.astype(BF16), win_ref[...])
    bg = y[:, :D_MODEL]
    prod = y[:, D_MODEL:2 * D_MODEL] * y[:, 2 * D_MODEL:]
    _causal_conv(prod, wconv_ref, hist_ref, conv_ref, width=CONV_C_WIDTH, pad=8)
    m = _dot((bg * conv_ref[...]).astype(BF16), wout_ref[...])
    o_ref[...] = _layer_norm(ALPHA * x + m, g1_ref[...], be1_ref[...])


def _ffn_kernel(x_ref, p_ref, wg_ref, wu_ref, wd_ref, g2_ref, be2_ref,
                wpg_ref, wpp_ref, gn_ref, o_ref):
    x = x_ref[...]
    xb = x.astype(BF16)
    hmid = jax.nn.silu(_dot(xb, wg_ref[...])) * _dot(xb, wu_ref[...])
    d = _dot(hmid.astype(BF16), wd_ref[...])
    x2 = _layer_norm(ALPHA * x + d, g2_ref[...], be2_ref[...])
    gate = jax.nn.sigmoid(_dot(x2.astype(BF16), wpg_ref[...]))
    pe = _dot(p_ref[...].astype(BF16), wpp_ref[...])
    pe = pe * lax.rsqrt(jnp.mean(pe * pe, axis=-1, keepdims=True) + LN_EPS) * gn_ref[...]
    o_ref[...] = x2 + gate * pe


def _row_spec(width):
    return pl.BlockSpec((TILE_M, width), lambda i: (i, 0))


def _resident_spec(shape):
    return pl.BlockSpec(shape, lambda i: (0,) * len(shape),
                        pipeline_mode=pl.Buffered(1))


def _call(body, name, row_inputs, resident_inputs, scratch_shapes=()):
    n_rows = row_inputs[0].shape[0]
    in_specs = ([_row_spec(a.shape[1]) for a in row_inputs]
                + [_resident_spec(a.shape) for a in resident_inputs])
    return pl.pallas_call(
        body,
        out_shape=jax.ShapeDtypeStruct((n_rows, D_MODEL), F32),
        grid=(n_rows // TILE_M,),
        in_specs=in_specs,
        out_specs=_row_spec(D_MODEL),
        scratch_shapes=list(scratch_shapes),
        compiler_params=pltpu.CompilerParams(
            dimension_semantics=("arbitrary",),
            vmem_limit_bytes=VMEM_LIMIT_BYTES),
        name=name,
    )(*row_inputs, *resident_inputs)


def _vec(a):
    return a.reshape(1, -1).astype(F32)


def _conv_scratch(pad):
    return (pltpu.VMEM((N_LANE_BLOCKS, pad + TILE_M, LANES), F32),
            pltpu.VMEM((TILE_M, D_MODEL), F32))


def kernel(x, p, a_w_pw1, a_b_pw1, a_w_dw, a_b_dw, a_ln_g, a_ln_b, a_w_pw2,
           b_w_in, b_b_in, b_ln_g, b_ln_b, b_w_s, b_b_s, b_w_out,
           c_w_in, c_w_conv, c_w_out, ln1_g, ln1_b, ln2_g, ln2_b,
           ffn_w_gate, ffn_w_up, ffn_w_down, ple_w_gate, ple_w_proj, ple_norm_g):
    assert x.shape == (BATCH, SEQ, D_MODEL) and p.shape == (DEPTH, BATCH, SEQ, D_PLE)
    n_rows = BATCH * SEQ
    xf = x.reshape(n_rows, D_MODEL)
    pf = p.reshape(DEPTH, n_rows, D_PLE)
    for i in range(DEPTH):
        m, j = i % N_MIXERS, i // N_MIXERS
        post = [_vec(ln1_g[i]), _vec(ln1_b[i])]
        if m == 0:
            xf = _call(
                _mixer_a_kernel, f"mixer_a_{i}", [xf],
                [a_w_pw1[j].astype(BF16), _vec(a_b_pw1[j]), a_w_dw[j].astype(F32),
                 _vec(a_b_dw[j]), _vec(a_ln_g[j]), _vec(a_ln_b[j]),
                 a_w_pw2[j].astype(BF16)] + post,
                _conv_scratch(32))
        elif m == 1:
            bias = jnp.repeat(jnp.transpose(b_b_s[j]).astype(F32), SGU_GROUP, axis=1)
            xf = _call(
                _mixer_b_kernel, f"mixer_b_{i}", [xf],
                [b_w_in[j].astype(BF16), _vec(b_b_in[j]), _vec(b_ln_g[j]),
                 _vec(b_ln_b[j]), b_w_s[j].astype(F32), bias,
                 b_w_out[j].astype(BF16)] + post,
                (pltpu.VMEM((TILE_M, SGU_WIDTH), BF16),))
        else:
            xf = _call(
                _mixer_c_kernel, f"mixer_c_{i}", [xf],
                [c_w_in[j].astype(BF16), c_w_conv[j].astype(F32),
                 c_w_out[j].astype(BF16)] + post,
                _conv_scratch(8))
        xf = _call(
            _ffn_kernel, f"ffn_{i}", [xf, pf[i]],
            [ffn_w_gate[i].astype(BF16), ffn_w_up[i].astype(BF16),
             ffn_w_down[i].astype(BF16), _vec(ln2_g[i]), _vec(ln2_b[i]),
             ple_w_gate[i].astype(BF16), ple_w_proj[i].astype(BF16),
             _vec(ple_norm_g[i])])
    return xf.reshape(BATCH, SEQ, D_MODEL)
```

```python
import functools

import jax
import jax.numpy as jnp
from jax import lax
from jax.experimental import pallas as pl
from jax.experimental.pallas import tpu as pltpu

D_MODEL = 1024
BATCH = 4
SEQ = 4096
DEPTH = 4
CHUNK = 64
N_MIXERS = 3
CONV_A_WIDTH = 31
CONV_C_WIDTH = 3
SGU_BLOCK = 128
SGU_HEADS = 8
SGU_WIDTH = 2 * D_MODEL
SGU_GROUP = SGU_WIDTH // SGU_HEADS
D_FF = -(-8 * D_MODEL // (3 * 256)) * 256
D_PLE = 256
ALPHA = (2 * DEPTH) ** 0.25
LN_EPS = 1e-5
N_ROWS = BATCH * SEQ

LANES = 128
SUBLANES = 8
N_LANE_BLOCKS = D_MODEL // LANES
TILE_M = 512
FUSED_TILE_M = 256
PIPELINE_LAG = 2
ROW_CHUNK = 32
CONV_A_PAD = 32
CONV_C_PAD = 8
VMEM_LIMIT_BYTES = 56 * 1024 * 1024

BF16 = jnp.bfloat16
F32 = jnp.float32
U32 = jnp.uint32


def _layer_norm(x, g, b):
    mu = jnp.mean(x, axis=-1, keepdims=True)
    xc = x - mu
    var = jnp.mean(xc * xc, axis=-1, keepdims=True)
    return xc * lax.rsqrt(var + LN_EPS) * g + b


def _dot(a, w):
    return jnp.dot(a, w, preferred_element_type=F32)


def _ffn_tail(x1, p, wg_ref, wu_ref, wd_ref, g2_ref, be2_ref, wpg_ref, wpp_ref, gn_ref):
    x1b = x1.astype(BF16)
    h = jax.nn.silu(_dot(x1b, wg_ref[...])) * _dot(x1b, wu_ref[...])
    d = _dot(h.astype(BF16), wd_ref[...])
    x2 = _layer_norm(ALPHA * x1 + d, g2_ref[...], be2_ref[...])
    gate = jax.nn.sigmoid(_dot(x2.astype(BF16), wpg_ref[...]))
    return x2 + gate * _embedding(p, wpp_ref, gn_ref)


def _embedding(p, wpp_ref, gn_ref):
    pe = _dot(p.astype(BF16), wpp_ref[...])
    return pe * lax.rsqrt(jnp.mean(pe * pe, axis=-1, keepdims=True) + LN_EPS) * gn_ref[...]


def _push_history(hist_ref, tile, tile_m, c, vals, pad):
    first = tile % (SEQ // tile_m) == 0
    tail = hist_ref[c, tile_m:tile_m + pad, :]
    hist_ref[c, 0:pad, :] = jnp.where(first, 0.0, tail)
    hist_ref[c, pad:pad + tile_m, :] = vals


def _conv_chunk(w_ref, hist_ref, c, r0, *, width, pad):
    base = pad - (width - 1)
    acc = None
    for k in range(width):
        win = hist_ref[c, pl.ds(r0 + base + k, ROW_CHUNK), :]
        term = win * w_ref[k:k + 1, c * LANES:(c + 1) * LANES]
        acc = term if acc is None else acc + term
    return acc


def _causal_conv_loop(w_ref, hist_ref, conv_ref, *, tile_m, width, pad):
    def chunk(r, carry):
        r0 = pl.multiple_of(r * ROW_CHUNK, ROW_CHUNK)
        for c in range(N_LANE_BLOCKS):
            conv_ref[pl.ds(r0, ROW_CHUNK), c * LANES:(c + 1) * LANES] = _conv_chunk(
                w_ref, hist_ref, c, r0, width=width, pad=pad)
        return carry

    lax.fori_loop(0, tile_m // ROW_CHUNK, chunk, 0)


def _mixer_b_kernel(x_ref, win_ref, bin_ref, lng_ref, lnb_ref, ws_ref, bs_ref,
                    wout_ref, g1_ref, be1_ref, o_ref, gated_ref):
    x = x_ref[...]
    z = jax.nn.gelu(_dot(x.astype(BF16), win_ref[...]) + bin_ref[...])
    u = z[:, :SGU_WIDTH]
    v = _layer_norm(z[:, SGU_WIDTH:], lng_ref[...], lnb_ref[...]).astype(BF16)
    row = lax.broadcasted_iota(jnp.int32, (SGU_BLOCK, SGU_BLOCK), 0) // CHUNK
    col = lax.broadcasted_iota(jnp.int32, (SGU_BLOCK, SGU_BLOCK), 1) // CHUNK
    mask = row >= col
    for hd in range(SGU_HEADS):
        w_m = jnp.where(mask, ws_ref[hd], 0.0).astype(BF16)
        cs = slice(hd * SGU_GROUP, (hd + 1) * SGU_GROUP)
        for n in range(TILE_M // SGU_BLOCK):
            rs = slice(n * SGU_BLOCK, (n + 1) * SGU_BLOCK)
            f = _dot(w_m, v[rs, cs]) + bs_ref[:, cs]
            gated_ref[rs, cs] = (u[rs, cs] * f).astype(BF16)
    m = _dot(gated_ref[...], wout_ref[...])
    o_ref[...] = _layer_norm(ALPHA * x + m, g1_ref[...], be1_ref[...])


def _mixer_c_kernel(x_ref, win_ref, wconv_ref, wout_ref, g1_ref, be1_ref,
                    o_ref, hist_ref, conv_ref):
    x = x_ref[...]
    y = _dot(x.astype(BF16), win_ref[...])
    prod = y[:, D_MODEL:2 * D_MODEL] * y[:, 2 * D_MODEL:]
    for c in range(N_LANE_BLOCKS):
        _push_history(hist_ref, pl.program_id(0), TILE_M, c,
                      prod[:, c * LANES:(c + 1) * LANES], CONV_C_PAD)
    _causal_conv_loop(wconv_ref, hist_ref, conv_ref, tile_m=TILE_M,
                      width=CONV_C_WIDTH, pad=CONV_C_PAD)
    m = _dot((y[:, :D_MODEL] * conv_ref[...]).astype(BF16), wout_ref[...])
    o_ref[...] = _layer_norm(ALPHA * x + m, g1_ref[...], be1_ref[...])


def _ffn_kernel(x_ref, p_ref, *refs):
    refs[-1][...] = _ffn_tail(x_ref[...], p_ref[...], *refs[:-1])


def _layer_of(stack, layer):
    return (stack, layer)


def _resident(operands):
    arrays, specs = [], []
    for op in operands:
        if isinstance(op, tuple):
            stack, layer = op
            shape = (None,) + stack.shape[1:]
            index = functools.partial(lambda l, n, i: (l,) + (0,) * n, layer, stack.ndim - 1)
            arrays.append(stack)
        else:
            shape = op.shape
            index = functools.partial(lambda n, i: (0,) * n, op.ndim)
            arrays.append(op)
        specs.append(pl.BlockSpec(shape, index, pipeline_mode=pl.Buffered(1)))
    return arrays, specs


def _row_call(body, name, x, p, resident, scratch_shapes=()):
    rows = [pl.BlockSpec((TILE_M, D_MODEL), lambda i: (i, 0))]
    args = [x]
    if p is not None:
        layer, pstack = p
        rows.append(pl.BlockSpec((None, TILE_M, D_PLE), lambda i: (layer, i, 0)))
        args.append(pstack)
    arrays, specs = _resident(resident)
    return pl.pallas_call(
        body,
        out_shape=jax.ShapeDtypeStruct((N_ROWS, D_MODEL), F32),
        grid=(N_ROWS // TILE_M,),
        in_specs=rows + specs,
        out_specs=pl.BlockSpec((TILE_M, D_MODEL), lambda i: (i, 0)),
        scratch_shapes=list(scratch_shapes),
        compiler_params=pltpu.CompilerParams(
            dimension_semantics=("arbitrary",),
            vmem_limit_bytes=VMEM_LIMIT_BYTES),
        name=name,
    )(*args, *arrays)


def _zero_of_last(v):
    bits = lax.bitcast_convert_type(v[-SUBLANES:, -LANES:], U32)
    return lax.shift_right_logical(lax.shift_right_logical(bits, U32(16)), U32(16))


def _as_f32_zero(zeros):
    return lax.bitcast_convert_type(functools.reduce(jnp.bitwise_or, zeros), F32)


def _tie(a, zeros):
    head = a[:SUBLANES, :LANES] + _as_f32_zero(zeros)
    top = jnp.concatenate([head, a[:SUBLANES, LANES:]], axis=1)
    return jnp.concatenate([top, a[SUBLANES:]], axis=0)


def _tie_all(a, zero):
    reps = (a.shape[0] // SUBLANES, a.shape[1] // LANES)
    return a + jnp.tile(_as_f32_zero([zero]), reps)


def _layer_a_kernel(x_ref, xold_ref, p_ref, w1_ref, b1_ref, wdw_ref, bdw_ref, lng_ref,
                    lnb_ref, w2_ref, g1_ref, be1_ref, *refs):
    wg_ref, wu_ref, wd_ref, g2_ref, be2_ref, wpg_ref, wpp_ref, gn_ref = refs[:8]
    o_ref, raw_ref, mid_ref, hist_ref, conv_ref = refs[8:]
    step = pl.program_id(0)

    @pl.when(step == 0)
    def _():
        for s in (raw_ref, mid_ref, hist_ref, conv_ref):
            s[...] = jnp.zeros(s.shape, s.dtype)

    slot = step % 2
    mid = mid_ref[...]

    half = D_MODEL
    raw_lo = _dot(x_ref[...].astype(BF16), w1_ref[:, :half])
    raw_ref[slot, :, :half] = raw_lo

    x1 = _layer_norm(ALPHA * xold_ref[...] + _dot(mid, w2_ref[...]), g1_ref[...], be1_ref[...])
    x1b = _tie(x1, [_zero_of_last(raw_lo)]).astype(BF16)
    h = jax.nn.silu(_dot(x1b, wg_ref[...])) * _dot(x1b, wu_ref[...])

    after_x1 = _zero_of_last(x1)
    for c in range(N_LANE_BLOCKS):
        lo = slice(c * LANES, (c + 1) * LANES)
        hi = slice(D_MODEL + c * LANES, D_MODEL + (c + 1) * LANES)
        a = raw_ref[1 - slot, :, lo] + b1_ref[:, lo]
        g = raw_ref[1 - slot, :, hi] + b1_ref[:, hi]
        _push_history(hist_ref, step - 1, FUSED_TILE_M, c,
                      _tie_all(a * jax.nn.sigmoid(g), after_x1), CONV_A_PAD)
    conv_zeros = []
    for r0 in range(0, FUSED_TILE_M, ROW_CHUNK):
        for c in range(N_LANE_BLOCKS):
            acc = _conv_chunk(wdw_ref, hist_ref, c, r0, width=CONV_A_WIDTH, pad=CONV_A_PAD)
            conv_ref[r0:r0 + ROW_CHUNK, c * LANES:(c + 1) * LANES] = acc
            conv_zeros.append(_zero_of_last(acc))

    y = _tie_all(conv_ref[...], _zero_of_last(h)) + bdw_ref[...]
    y = jax.nn.silu(_layer_norm(y, lng_ref[...], lnb_ref[...]))
    mid_ref[...] = y.astype(BF16)
    norm_zeros = [_zero_of_last(y[r:r + SUBLANES]) for r in range(0, FUSED_TILE_M, ROW_CHUNK)]

    d = _dot(h.astype(BF16), wd_ref[...])
    raw_hi = _dot(_tie(x_ref[...], [_zero_of_last(d)]).astype(BF16), w1_ref[:, half:])
    raw_ref[slot, :, half:] = raw_hi
    x2 = _layer_norm(ALPHA * x1 + d, g2_ref[...], be2_ref[...])
    x2b = _tie(x2, [_zero_of_last(raw_hi)] + conv_zeros).astype(BF16)
    gate = _tie(jax.nn.sigmoid(_dot(x2b, wpg_ref[...])), norm_zeros)
    o_ref[...] = x2 + gate * _embedding(p_ref[...], wpp_ref, gn_ref)


def _layer_a_call(name, layer, x, pstack, resident):
    n_tiles = N_ROWS // FUSED_TILE_M
    new = lambda i: (jnp.minimum(i, n_tiles - 1), 0)
    old = lambda i: (jnp.maximum(i - PIPELINE_LAG, 0), 0)
    old_p = lambda i: (layer, jnp.maximum(i - PIPELINE_LAG, 0), 0)
    arrays, specs = _resident(resident)
    return pl.pallas_call(
        _layer_a_kernel,
        out_shape=jax.ShapeDtypeStruct((N_ROWS, D_MODEL), F32),
        grid=(n_tiles + PIPELINE_LAG,),
        in_specs=[pl.BlockSpec((FUSED_TILE_M, D_MODEL), new),
                  pl.BlockSpec((FUSED_TILE_M, D_MODEL), old),
                  pl.BlockSpec((None, FUSED_TILE_M, D_PLE), old_p)] + specs,
        out_specs=pl.BlockSpec((FUSED_TILE_M, D_MODEL), old),
        scratch_shapes=[pltpu.VMEM((2, FUSED_TILE_M, 2 * D_MODEL), F32),
                        pltpu.VMEM((FUSED_TILE_M, D_MODEL), BF16),
                        *_conv_scratch(FUSED_TILE_M, CONV_A_PAD)],
        compiler_params=pltpu.CompilerParams(
            dimension_semantics=("arbitrary",),
            vmem_limit_bytes=VMEM_LIMIT_BYTES),
        name=name,
    )(x, x, pstack, *arrays)


def _conv_scratch(tile_m, pad):
    return (pltpu.VMEM((N_LANE_BLOCKS, pad + tile_m, LANES), F32),
            pltpu.VMEM((tile_m, D_MODEL), F32))


def _vec(a):
    return a.reshape(1, -1).astype(F32)


def kernel(x, p, a_w_pw1, a_b_pw1, a_w_dw, a_b_dw, a_ln_g, a_ln_b, a_w_pw2,
           b_w_in, b_b_in, b_ln_g, b_ln_b, b_w_s, b_b_s, b_w_out,
           c_w_in, c_w_conv, c_w_out, ln1_g, ln1_b, ln2_g, ln2_b,
           ffn_w_gate, ffn_w_up, ffn_w_down, ple_w_gate, ple_w_proj, ple_norm_g):
    assert x.shape == (BATCH, SEQ, D_MODEL) and p.shape == (DEPTH, BATCH, SEQ, D_PLE)
    xf = x.reshape(N_ROWS, D_MODEL)
    pstack = p.reshape(DEPTH, N_ROWS, D_PLE)
    (a_w_pw1, a_w_pw2, b_w_in, b_w_out, c_w_in, c_w_out, ffn_w_gate, ffn_w_up,
     ffn_w_down, ple_w_gate, ple_w_proj) = (
        w.astype(BF16) for w in (a_w_pw1, a_w_pw2, b_w_in, b_w_out, c_w_in, c_w_out,
                                 ffn_w_gate, ffn_w_up, ffn_w_down, ple_w_gate, ple_w_proj))
    for i in range(DEPTH):
        m, j = i % N_MIXERS, i // N_MIXERS
        post = [_vec(ln1_g[i]), _vec(ln1_b[i])]
        ffn = [_layer_of(ffn_w_gate, i), _layer_of(ffn_w_up, i), _layer_of(ffn_w_down, i),
               _vec(ln2_g[i]), _vec(ln2_b[i]), _layer_of(ple_w_gate, i),
               _layer_of(ple_w_proj, i), _vec(ple_norm_g[i])]
        if m == 0:
            xf = _layer_a_call(
                f"layer_a_{i}", i, xf, pstack,
                [_layer_of(a_w_pw1, j), _vec(a_b_pw1[j]), _layer_of(a_w_dw.astype(F32), j),
                 _vec(a_b_dw[j]), _vec(a_ln_g[j]), _vec(a_ln_b[j]),
                 _layer_of(a_w_pw2, j)] + post + ffn)
            continue
        if m == 1:
            bias = jnp.repeat(jnp.transpose(b_b_s[j]).astype(F32), SGU_GROUP, axis=1)
            xf = _row_call(
                _mixer_b_kernel, f"mixer_b_{i}", xf, None,
                [_layer_of(b_w_in, j), _vec(b_b_in[j]), _vec(b_ln_g[j]), _vec(b_ln_b[j]),
                 _layer_of(b_w_s.astype(F32), j), bias, _layer_of(b_w_out, j)] + post,
                (pltpu.VMEM((TILE_M, SGU_WIDTH), BF16),))
        else:
            xf = _row_call(
                _mixer_c_kernel, f"mixer_c_{i}", xf, None,
                [_layer_of(c_w_in, j), _layer_of(c_w_conv.astype(F32), j),
                 _layer_of(c_w_out, j)] + post,
                _conv_scratch(TILE_M, CONV_C_PAD))
        xf = _row_call(_ffn_kernel, f"ffn_{i}", xf, (i, pstack), ffn)
    return xf.reshape(BATCH, SEQ, D_MODEL)
```

```python
import functools

import jax
import jax.numpy as jnp
from jax import lax
from jax.experimental import pallas as pl
from jax.experimental.pallas import tpu as pltpu

D_MODEL = 1024
BATCH = 4
SEQ = 4096
DEPTH = 4
CHUNK = 64
N_MIXERS = 3
CONV_A_WIDTH = 31
CONV_C_WIDTH = 3
SGU_BLOCK = 128
SGU_HEADS = 8
SGU_WIDTH = 2 * D_MODEL
SGU_GROUP = SGU_WIDTH // SGU_HEADS
D_FF = -(-8 * D_MODEL // (3 * 256)) * 256
D_PLE = 256
ALPHA = (2 * DEPTH) ** 0.25
LN_EPS = 1e-5
N_ROWS = BATCH * SEQ

LANES = 128
SUBLANES = 8
N_LANE_BLOCKS = D_MODEL // LANES
TILE_M = 512
FUSED_TILE_M = 256
PIPELINE_LAG = 2
ROW_CHUNK = 32
CONV_A_PAD = 32
CONV_C_PAD = 8
VMEM_LIMIT_BYTES = 56 * 1024 * 1024

BF16 = jnp.bfloat16
F32 = jnp.float32
U32 = jnp.uint32


def _layer_norm(x, g, b):
    mu = jnp.mean(x, axis=-1, keepdims=True)
    xc = x - mu
    var = jnp.mean(xc * xc, axis=-1, keepdims=True)
    return xc * lax.rsqrt(var + LN_EPS) * g + b


def _dot(a, w):
    return jnp.dot(a, w, preferred_element_type=F32)


def _embedding(p, wpp_ref, gn_ref):
    pe = _dot(p.astype(BF16), wpp_ref[...])
    return pe * lax.rsqrt(jnp.mean(pe * pe, axis=-1, keepdims=True) + LN_EPS) * gn_ref[...]


def _push_history(hist_ref, tile, tile_m, c, vals, pad):
    first = tile % (SEQ // tile_m) == 0
    tail = hist_ref[c, tile_m:tile_m + pad, :]
    hist_ref[c, 0:pad, :] = jnp.where(first, 0.0, tail)
    hist_ref[c, pad:pad + tile_m, :] = vals


def _conv_chunk(w_ref, hist_ref, c, r0, *, width, pad):
    base = pad - (width - 1)
    acc = None
    for k in range(width):
        win = hist_ref[c, pl.ds(r0 + base + k, ROW_CHUNK), :]
        term = win * w_ref[k:k + 1, c * LANES:(c + 1) * LANES]
        acc = term if acc is None else acc + term
    return acc


HALF_M = TILE_M // 2


def _mixer_b_half(rows, x_ref, win_ref, bin_ref, lng_ref, lnb_ref, ws_ref, bs_ref,
                  wout_ref, g1_ref, be1_ref, o_ref, gated_ref):
    x = x_ref[rows, :]
    xb = x.astype(BF16)
    zv = _dot(xb, win_ref[:, SGU_WIDTH:])
    yield
    v = jax.nn.gelu(zv + bin_ref[:, SGU_WIDTH:])
    v = _layer_norm(v, lng_ref[...], lnb_ref[...]).astype(BF16)
    yield
    zu = _dot(xb, win_ref[:, :SGU_WIDTH])
    yield
    u = jax.nn.gelu(zu + bin_ref[:, :SGU_WIDTH])
    row = lax.broadcasted_iota(jnp.int32, (SGU_BLOCK, SGU_BLOCK), 0) // CHUNK
    col = lax.broadcasted_iota(jnp.int32, (SGU_BLOCK, SGU_BLOCK), 1) // CHUNK
    mask = row >= col
    for hd in range(SGU_HEADS):
        w_m = jnp.where(mask, ws_ref[hd], 0.0).astype(BF16)
        cs = slice(hd * SGU_GROUP, (hd + 1) * SGU_GROUP)
        for r0 in range(0, HALF_M, SGU_BLOCK):
            f = _dot(w_m, v[r0:r0 + SGU_BLOCK, cs]) + bs_ref[:, cs]
            gated_ref[rows.start + r0:rows.start + r0 + SGU_BLOCK, cs] = (
                u[r0:r0 + SGU_BLOCK, cs] * f).astype(BF16)
    yield
    m = _dot(gated_ref[rows, :], wout_ref[...])
    yield
    o_ref[rows, :] = _layer_norm(ALPHA * x + m, g1_ref[...], be1_ref[...])
    yield


def _mixer_b_kernel(*refs):
    h0 = _mixer_b_half(slice(0, HALF_M), *refs)
    h1 = _mixer_b_half(slice(HALF_M, TILE_M), *refs)
    for half in (h0, h1, h0, h0, h1, h1, h0, h0, h1, h1, h0, h1):
        next(half)


def _mixer_c_kernel(x_ref, win_ref, wconv_ref, wout_ref, g1_ref, be1_ref,
                    o_ref, hist_ref, conv_ref):
    x = x_ref[...]
    xb = x.astype(BF16)
    y = _dot(xb, win_ref[:, D_MODEL:])
    prod = y[:, :D_MODEL] * y[:, D_MODEL:]
    for c in range(N_LANE_BLOCKS):
        _push_history(hist_ref, pl.program_id(0), TILE_M, c,
                      prod[:, c * LANES:(c + 1) * LANES], CONV_C_PAD)
    for r0 in range(0, TILE_M, ROW_CHUNK):
        for c in range(N_LANE_BLOCKS):
            conv_ref[r0:r0 + ROW_CHUNK, c * LANES:(c + 1) * LANES] = _conv_chunk(
                wconv_ref, hist_ref, c, r0, width=CONV_C_WIDTH, pad=CONV_C_PAD)
    bg = _dot(xb, win_ref[:, :D_MODEL])
    m = _dot((bg * conv_ref[...]).astype(BF16), wout_ref[...])
    o_ref[...] = _layer_norm(ALPHA * x + m, g1_ref[...], be1_ref[...])


def _ffn_half(rows, x_ref, p_ref, wg_ref, wu_ref, wd_ref, g2_ref, be2_ref, wpg_ref,
              wpp_ref, gn_ref, o_ref):
    x1 = x_ref[rows, :]
    x1b = x1.astype(BF16)
    h = jax.nn.silu(_dot(x1b, wg_ref[...])) * _dot(x1b, wu_ref[...])
    yield
    d = _dot(h.astype(BF16), wd_ref[...])
    yield
    x2 = _layer_norm(ALPHA * x1 + d, g2_ref[...], be2_ref[...])
    gate = jax.nn.sigmoid(_dot(x2.astype(BF16), wpg_ref[...]))
    o_ref[rows, :] = x2 + gate * _embedding(p_ref[rows, :], wpp_ref, gn_ref)
    yield


def _ffn_kernel(*refs):
    h0 = _ffn_half(slice(0, HALF_M), *refs)
    h1 = _ffn_half(slice(HALF_M, TILE_M), *refs)
    for half in (h0, h0, h1, h0, h1, h1):
        next(half)


def _layer_of(stack, layer):
    return (stack, layer)


def _resident(operands):
    arrays, specs = [], []
    for op in operands:
        if isinstance(op, tuple):
            stack, layer = op
            shape = (None,) + stack.shape[1:]
            index = functools.partial(lambda l, n, i: (l,) + (0,) * n, layer, stack.ndim - 1)
            arrays.append(stack)
        else:
            shape = op.shape
            index = functools.partial(lambda n, i: (0,) * n, op.ndim)
            arrays.append(op)
        specs.append(pl.BlockSpec(shape, index, pipeline_mode=pl.Buffered(1)))
    return arrays, specs


def _row_call(body, name, x, p, resident, scratch_shapes=()):
    rows = [pl.BlockSpec((TILE_M, D_MODEL), lambda i: (i, 0))]
    args = [x]
    if p is not None:
        layer, pstack = p
        rows.append(pl.BlockSpec((None, TILE_M, D_PLE), lambda i: (layer, i, 0)))
        args.append(pstack)
    arrays, specs = _resident(resident)
    return pl.pallas_call(
        body,
        out_shape=jax.ShapeDtypeStruct((N_ROWS, D_MODEL), F32),
        grid=(N_ROWS // TILE_M,),
        in_specs=rows + specs,
        out_specs=pl.BlockSpec((TILE_M, D_MODEL), lambda i: (i, 0)),
        scratch_shapes=list(scratch_shapes),
        compiler_params=pltpu.CompilerParams(
            dimension_semantics=("arbitrary",),
            vmem_limit_bytes=VMEM_LIMIT_BYTES),
        name=name,
    )(*args, *arrays)


def _zero_of_last(v):
    bits = lax.bitcast_convert_type(v[-SUBLANES:, -LANES:], U32)
    return lax.shift_right_logical(lax.shift_right_logical(bits, U32(16)), U32(16))


def _as_f32_zero(zeros):
    return lax.bitcast_convert_type(functools.reduce(jnp.bitwise_or, zeros), F32)


def _tie(a, zeros):
    head = a[:SUBLANES, :LANES] + _as_f32_zero(zeros)
    top = jnp.concatenate([head, a[:SUBLANES, LANES:]], axis=1)
    return jnp.concatenate([top, a[SUBLANES:]], axis=0)


def _tie_all(a, zero):
    reps = (a.shape[0] // SUBLANES, a.shape[1] // LANES)
    return a + jnp.tile(_as_f32_zero([zero]), reps)


def _layer_a_kernel(x_ref, xold_ref, p_ref, w1_ref, b1_ref, wdw_ref, bdw_ref, lng_ref,
                    lnb_ref, w2_ref, g1_ref, be1_ref, *refs):
    wg_ref, wu_ref, wd_ref, g2_ref, be2_ref, wpg_ref, wpp_ref, gn_ref = refs[:8]
    o_ref, raw_ref, mid_ref, hist_ref, conv_ref = refs[8:]
    step = pl.program_id(0)

    @pl.when(step == 0)
    def _():
        for s in (raw_ref, mid_ref, hist_ref, conv_ref):
            s[...] = jnp.zeros(s.shape, s.dtype)

    slot = step % 2
    mid = mid_ref[...]

    half = D_MODEL
    raw_lo = _dot(x_ref[...].astype(BF16), w1_ref[:, :half])
    raw_ref[slot, :, :half] = raw_lo

    x1 = _layer_norm(ALPHA * xold_ref[...] + _dot(mid, w2_ref[...]), g1_ref[...], be1_ref[...])
    x1b = _tie(x1, [_zero_of_last(raw_lo)]).astype(BF16)
    h = jax.nn.silu(_dot(x1b, wg_ref[...])) * _dot(x1b, wu_ref[...])

    after_x1 = _zero_of_last(x1)
    for c in range(N_LANE_BLOCKS):
        lo = slice(c * LANES, (c + 1) * LANES)
        hi = slice(D_MODEL + c * LANES, D_MODEL + (c + 1) * LANES)
        a = raw_ref[1 - slot, :, lo] + b1_ref[:, lo]
        g = raw_ref[1 - slot, :, hi] + b1_ref[:, hi]
        _push_history(hist_ref, step - 1, FUSED_TILE_M, c,
                      _tie_all(a * jax.nn.sigmoid(g), after_x1), CONV_A_PAD)
    conv_zeros = []
    for r0 in range(0, FUSED_TILE_M, ROW_CHUNK):
        for c in range(N_LANE_BLOCKS):
            acc = _conv_chunk(wdw_ref, hist_ref, c, r0, width=CONV_A_WIDTH, pad=CONV_A_PAD)
            conv_ref[r0:r0 + ROW_CHUNK, c * LANES:(c + 1) * LANES] = acc
            conv_zeros.append(_zero_of_last(acc))

    y = _tie_all(conv_ref[...], _zero_of_last(h)) + bdw_ref[...]
    y = jax.nn.silu(_layer_norm(y, lng_ref[...], lnb_ref[...]))
    mid_ref[...] = y.astype(BF16)
    norm_zeros = [_zero_of_last(y[r:r + SUBLANES]) for r in range(0, FUSED_TILE_M, ROW_CHUNK)]

    d = _dot(h.astype(BF16), wd_ref[...])
    raw_hi = _dot(_tie(x_ref[...], [_zero_of_last(d)]).astype(BF16), w1_ref[:, half:])
    raw_ref[slot, :, half:] = raw_hi
    x2 = _layer_norm(ALPHA * x1 + d, g2_ref[...], be2_ref[...])
    x2b = _tie(x2, [_zero_of_last(raw_hi)] + conv_zeros).astype(BF16)
    gate = _tie(jax.nn.sigmoid(_dot(x2b, wpg_ref[...])), norm_zeros)
    o_ref[...] = x2 + gate * _embedding(p_ref[...], wpp_ref, gn_ref)


def _layer_a_call(name, layer, x, pstack, resident):
    n_tiles = N_ROWS // FUSED_TILE_M
    new = lambda i: (jnp.minimum(i, n_tiles - 1), 0)
    old = lambda i: (jnp.maximum(i - PIPELINE_LAG, 0), 0)
    old_p = lambda i: (layer, jnp.maximum(i - PIPELINE_LAG, 0), 0)
    arrays, specs = _resident(resident)
    return pl.pallas_call(
        _layer_a_kernel,
        out_shape=jax.ShapeDtypeStruct((N_ROWS, D_MODEL), F32),
        grid=(n_tiles + PIPELINE_LAG,),
        in_specs=[pl.BlockSpec((FUSED_TILE_M, D_MODEL), new),
                  pl.BlockSpec((FUSED_TILE_M, D_MODEL), old),
                  pl.BlockSpec((None, FUSED_TILE_M, D_PLE), old_p)] + specs,
        out_specs=pl.BlockSpec((FUSED_TILE_M, D_MODEL), old),
        scratch_shapes=[pltpu.VMEM((2, FUSED_TILE_M, 2 * D_MODEL), F32),
                        pltpu.VMEM((FUSED_TILE_M, D_MODEL), BF16),
                        *_conv_scratch(FUSED_TILE_M, CONV_A_PAD)],
        compiler_params=pltpu.CompilerParams(
            dimension_semantics=("arbitrary",),
            vmem_limit_bytes=VMEM_LIMIT_BYTES),
        name=name,
    )(x, x, pstack, *arrays)


def _conv_scratch(tile_m, pad):
    return (pltpu.VMEM((N_LANE_BLOCKS, pad + tile_m, LANES), F32),
            pltpu.VMEM((tile_m, D_MODEL), F32))


def _vec(a):
    return a.reshape(1, -1).astype(F32)


def kernel(x, p, a_w_pw1, a_b_pw1, a_w_dw, a_b_dw, a_ln_g, a_ln_b, a_w_pw2,
           b_w_in, b_b_in, b_ln_g, b_ln_b, b_w_s, b_b_s, b_w_out,
           c_w_in, c_w_conv, c_w_out, ln1_g, ln1_b, ln2_g, ln2_b,
           ffn_w_gate, ffn_w_up, ffn_w_down, ple_w_gate, ple_w_proj, ple_norm_g):
    assert x.shape == (BATCH, SEQ, D_MODEL) and p.shape == (DEPTH, BATCH, SEQ, D_PLE)
    xf = x.reshape(N_ROWS, D_MODEL)
    pstack = p.reshape(DEPTH, N_ROWS, D_PLE)
    (a_w_pw1, a_w_pw2, b_w_in, b_w_out, c_w_in, c_w_out, ffn_w_gate, ffn_w_up,
     ffn_w_down, ple_w_gate, ple_w_proj) = (
        w.astype(BF16) for w in (a_w_pw1, a_w_pw2, b_w_in, b_w_out, c_w_in, c_w_out,
                                 ffn_w_gate, ffn_w_up, ffn_w_down, ple_w_gate, ple_w_proj))
    for i in range(DEPTH):
        m, j = i % N_MIXERS, i // N_MIXERS
        post = [_vec(ln1_g[i]), _vec(ln1_b[i])]
        ffn = [_layer_of(ffn_w_gate, i), _layer_of(ffn_w_up, i), _layer_of(ffn_w_down, i),
               _vec(ln2_g[i]), _vec(ln2_b[i]), _layer_of(ple_w_gate, i),
               _layer_of(ple_w_proj, i), _vec(ple_norm_g[i])]
        if m == 0:
            xf = _layer_a_call(
                f"layer_a_{i}", i, xf, pstack,
                [_layer_of(a_w_pw1, j), _vec(a_b_pw1[j]), _layer_of(a_w_dw.astype(F32), j),
                 _vec(a_b_dw[j]), _vec(a_ln_g[j]), _vec(a_ln_b[j]),
                 _layer_of(a_w_pw2, j)] + post + ffn)
            continue
        if m == 1:
            bias = jnp.repeat(jnp.transpose(b_b_s[j]).astype(F32), SGU_GROUP, axis=1)
            xf = _row_call(
                _mixer_b_kernel, f"mixer_b_{i}", xf, None,
                [_layer_of(b_w_in, j), _vec(b_b_in[j]), _vec(b_ln_g[j]), _vec(b_ln_b[j]),
                 _layer_of(b_w_s.astype(F32), j), bias, _layer_of(b_w_out, j)] + post,
                (pltpu.VMEM((TILE_M, SGU_WIDTH), BF16),))
        else:
            xf = _row_call(
                _mixer_c_kernel, f"mixer_c_{i}", xf, None,
                [_layer_of(c_w_in, j), _layer_of(c_w_conv.astype(F32), j),
                 _layer_of(c_w_out, j)] + post,
                _conv_scratch(TILE_M, CONV_C_PAD))
        xf = _row_call(_ffn_kernel, f"ffn_{i}", xf, (i, pstack), ffn)
    return xf.reshape(BATCH, SEQ, D_MODEL)
```

```python
import functools

import jax
import jax.numpy as jnp
from jax import lax
from jax.experimental import pallas as pl
from jax.experimental.pallas import tpu as pltpu

D_MODEL = 1024
BATCH = 4
SEQ = 4096
DEPTH = 4
CHUNK = 64
N_MIXERS = 3
CONV_A_WIDTH = 31
CONV_C_WIDTH = 3
SGU_BLOCK = 128
SGU_HEADS = 8
SGU_WIDTH = 2 * D_MODEL
SGU_GROUP = SGU_WIDTH // SGU_HEADS
D_FF = -(-8 * D_MODEL // (3 * 256)) * 256
D_PLE = 256
ALPHA = (2 * DEPTH) ** 0.25
LN_EPS = 1e-5
N_ROWS = BATCH * SEQ

LANES = 128
SUBLANES = 8
N_LANE_BLOCKS = D_MODEL // LANES
TILE_M = 512
FUSED_TILE_M = 256
PIPELINE_LAG = 2
ROW_CHUNK = 32
CONV_A_PAD = 32
CONV_C_PAD = 8
VMEM_LIMIT_BYTES = 56 * 1024 * 1024

BF16 = jnp.bfloat16
F32 = jnp.float32
U32 = jnp.uint32


def _layer_norm(x, g, b):
    mu = jnp.mean(x, axis=-1, keepdims=True)
    xc = x - mu
    var = jnp.mean(xc * xc, axis=-1, keepdims=True)
    return xc * lax.rsqrt(var + LN_EPS) * g + b


def _dot(a, w):
    return jnp.dot(a, w, preferred_element_type=F32)


def _embedding(p, wpp_ref, gn_ref):
    pe = _dot(p.astype(BF16), wpp_ref[...])
    return pe * lax.rsqrt(jnp.mean(pe * pe, axis=-1, keepdims=True) + LN_EPS) * gn_ref[...]


def _push_history(hist_ref, tile, tile_m, c, vals, pad):
    first = tile % (SEQ // tile_m) == 0
    tail = hist_ref[c, tile_m:tile_m + pad, :]
    hist_ref[c, 0:pad, :] = jnp.where(first, 0.0, tail)
    hist_ref[c, pad:pad + tile_m, :] = vals


def _conv_chunk(w_ref, hist_ref, c, r0, *, width, pad):
    base = pad - (width - 1)
    acc = None
    for k in range(width):
        win = hist_ref[c, pl.ds(r0 + base + k, ROW_CHUNK), :]
        term = win * w_ref[k:k + 1, c * LANES:(c + 1) * LANES]
        acc = term if acc is None else acc + term
    return acc


HALF_M = TILE_M // 2


def _mixer_b_half(rows, x_ref, win_ref, bin_ref, lng_ref, lnb_ref, ws_ref, bs_ref,
                  wout_ref, g1_ref, be1_ref, o_ref, gated_ref):
    x = x_ref[rows, :]
    xb = x.astype(BF16)
    zv = _dot(xb, win_ref[:, SGU_WIDTH:])
    yield
    v = jax.nn.gelu(zv + bin_ref[:, SGU_WIDTH:])
    v = _layer_norm(v, lng_ref[...], lnb_ref[...]).astype(BF16)
    yield
    zu = _dot(xb, win_ref[:, :SGU_WIDTH])
    yield
    u = jax.nn.gelu(zu + bin_ref[:, :SGU_WIDTH])
    row = lax.broadcasted_iota(jnp.int32, (SGU_BLOCK, SGU_BLOCK), 0) // CHUNK
    col = lax.broadcasted_iota(jnp.int32, (SGU_BLOCK, SGU_BLOCK), 1) // CHUNK
    mask = row >= col
    for hd in range(SGU_HEADS):
        w_m = jnp.where(mask, ws_ref[hd], 0.0).astype(BF16)
        cs = slice(hd * SGU_GROUP, (hd + 1) * SGU_GROUP)
        for r0 in range(0, HALF_M, SGU_BLOCK):
            f = _dot(w_m, v[r0:r0 + SGU_BLOCK, cs]) + bs_ref[:, cs]
            gated_ref[rows.start + r0:rows.start + r0 + SGU_BLOCK, cs] = (
                u[r0:r0 + SGU_BLOCK, cs] * f).astype(BF16)
    yield
    m = _dot(gated_ref[rows, :], wout_ref[...])
    yield
    o_ref[rows, :] = _layer_norm(ALPHA * x + m, g1_ref[...], be1_ref[...])
    yield


def _mixer_b_kernel(*refs):
    h0 = _mixer_b_half(slice(0, HALF_M), *refs)
    h1 = _mixer_b_half(slice(HALF_M, TILE_M), *refs)
    for half in (h0, h1, h0, h0, h1, h1, h0, h0, h1, h1, h0, h1):
        next(half)


def _mixer_c_kernel(x_ref, win_ref, wconv_ref, wout_ref, g1_ref, be1_ref,
                    o_ref, hist_ref, conv_ref):
    x = x_ref[...]
    xb = x.astype(BF16)
    y = _dot(xb, win_ref[:, D_MODEL:])
    prod = y[:, :D_MODEL] * y[:, D_MODEL:]
    for c in range(N_LANE_BLOCKS):
        _push_history(hist_ref, pl.program_id(0), TILE_M, c,
                      prod[:, c * LANES:(c + 1) * LANES], CONV_C_PAD)
    for r0 in range(0, TILE_M, ROW_CHUNK):
        for c in range(N_LANE_BLOCKS):
            conv_ref[r0:r0 + ROW_CHUNK, c * LANES:(c + 1) * LANES] = _conv_chunk(
                wconv_ref, hist_ref, c, r0, width=CONV_C_WIDTH, pad=CONV_C_PAD)
    bg = _dot(xb, win_ref[:, :D_MODEL])
    m = _dot((bg * conv_ref[...]).astype(BF16), wout_ref[...])
    o_ref[...] = _layer_norm(ALPHA * x + m, g1_ref[...], be1_ref[...])


def _ffn_half(rows, x_ref, p_ref, wg_ref, wu_ref, wd_ref, g2_ref, be2_ref, wpg_ref,
              wpp_ref, gn_ref, o_ref):
    x1 = x_ref[rows, :]
    x1b = x1.astype(BF16)
    h = jax.nn.silu(_dot(x1b, wg_ref[...])) * _dot(x1b, wu_ref[...])
    yield
    d = _dot(h.astype(BF16), wd_ref[...])
    yield
    x2 = _layer_norm(ALPHA * x1 + d, g2_ref[...], be2_ref[...])
    gate = jax.nn.sigmoid(_dot(x2.astype(BF16), wpg_ref[...]))
    o_ref[rows, :] = x2 + gate * _embedding(p_ref[rows, :], wpp_ref, gn_ref)
    yield


def _ffn_kernel(*refs):
    h0 = _ffn_half(slice(0, HALF_M), *refs)
    h1 = _ffn_half(slice(HALF_M, TILE_M), *refs)
    for half in (h0, h0, h1, h0, h1, h1):
        next(half)


def _layer_of(stack, layer):
    return (stack, layer)


def _resident(operands):
    arrays, specs = [], []
    for op in operands:
        if isinstance(op, tuple):
            stack, layer = op
            shape = (None,) + stack.shape[1:]
            index = functools.partial(lambda l, n, i: (l,) + (0,) * n, layer, stack.ndim - 1)
            arrays.append(stack)
        else:
            shape = op.shape
            index = functools.partial(lambda n, i: (0,) * n, op.ndim)
            arrays.append(op)
        specs.append(pl.BlockSpec(shape, index, pipeline_mode=pl.Buffered(1)))
    return arrays, specs


BF16_ROWS = 16


def _cast_specs(casts, n_steps):
    arrays, in_specs, out_specs, out_shapes, spbs = [], [], [], [], []
    for stack, layer in casts:
        k, n = stack.shape[1:]
        n_blocks = n_steps
        while k % (n_blocks * BF16_ROWS):
            n_blocks //= 2
        spb = n_steps // n_blocks
        block = functools.partial(lambda spb, nb, i: jnp.minimum(i // spb, nb - 1), spb, n_blocks)
        arrays.append(stack)
        in_specs.append(pl.BlockSpec(
            (None, k // n_blocks, n),
            functools.partial(lambda block, layer, i: (layer, block(i), 0), block, layer)))
        out_specs.append(pl.BlockSpec(
            (k // n_blocks, n), functools.partial(lambda block, i: (block(i), 0), block)))
        out_shapes.append(jax.ShapeDtypeStruct((k, n), BF16))
        spbs.append(spb)
    return arrays, in_specs, out_specs, out_shapes, spbs


def _with_casts(body, n_in, spbs):
    n = len(spbs)

    def kernel(*refs):
        ins, srcs = refs[:n_in], refs[n_in:n_in + n]
        out, dsts, scratch = refs[n_in + n], refs[n_in + n + 1:n_in + 2 * n + 1], refs[n_in + 2 * n + 1:]
        for spb in sorted(set(spbs)):
            @pl.when(pl.program_id(0) % spb == 0)
            def _():
                for src, dst, s in zip(srcs, dsts, spbs):
                    if s == spb:
                        dst[...] = src[...].astype(BF16)
        body(*ins, out, *scratch)

    return kernel


def _row_call(body, name, x, p, resident, scratch_shapes=(), casts=()):
    rows = [pl.BlockSpec((TILE_M, D_MODEL), lambda i: (i, 0))]
    args = [x]
    if p is not None:
        layer, pstack = p
        rows.append(pl.BlockSpec((None, TILE_M, D_PLE), lambda i: (layer, i, 0)))
        args.append(pstack)
    arrays, specs = _resident(resident)
    n_steps = N_ROWS // TILE_M
    c_arrays, c_in, c_out, c_shapes, spbs = _cast_specs(casts, n_steps)
    out, *cast = pl.pallas_call(
        _with_casts(body, len(args) + len(arrays), spbs),
        out_shape=[jax.ShapeDtypeStruct((N_ROWS, D_MODEL), F32)] + c_shapes,
        grid=(n_steps,),
        in_specs=rows + specs + c_in,
        out_specs=[pl.BlockSpec((TILE_M, D_MODEL), lambda i: (i, 0))] + c_out,
        scratch_shapes=list(scratch_shapes),
        compiler_params=pltpu.CompilerParams(
            dimension_semantics=("arbitrary",),
            vmem_limit_bytes=VMEM_LIMIT_BYTES),
        name=name,
    )(*args, *arrays, *c_arrays)
    return out, cast


def _zero_of_last(v):
    bits = lax.bitcast_convert_type(v[-SUBLANES:, -LANES:], U32)
    return lax.shift_right_logical(lax.shift_right_logical(bits, U32(16)), U32(16))


def _as_f32_zero(zeros):
    return lax.bitcast_convert_type(functools.reduce(jnp.bitwise_or, zeros), F32)


def _tie(a, zeros):
    head = a[:SUBLANES, :LANES] + _as_f32_zero(zeros)
    top = jnp.concatenate([head, a[:SUBLANES, LANES:]], axis=1)
    return jnp.concatenate([top, a[SUBLANES:]], axis=0)


def _tie_all(a, zero):
    reps = (a.shape[0] // SUBLANES, a.shape[1] // LANES)
    return a + jnp.tile(_as_f32_zero([zero]), reps)


def _layer_a_kernel(x_ref, xold_ref, p_ref, w1_ref, b1_ref, wdw_ref, bdw_ref, lng_ref,
                    lnb_ref, w2_ref, g1_ref, be1_ref, *refs):
    wg_ref, wu_ref, wd_ref, g2_ref, be2_ref, wpg_ref, wpp_ref, gn_ref = refs[:8]
    o_ref, raw_ref, mid_ref, hist_ref, conv_ref = refs[8:]
    step = pl.program_id(0)

    @pl.when(step == 0)
    def _():
        for s in (raw_ref, mid_ref, hist_ref, conv_ref):
            s[...] = jnp.zeros(s.shape, s.dtype)

    slot = step % 2
    mid = mid_ref[...]

    half = D_MODEL
    raw_lo = _dot(x_ref[...].astype(BF16), w1_ref[:, :half])
    raw_ref[slot, :, :half] = raw_lo

    x1 = _layer_norm(ALPHA * xold_ref[...] + _dot(mid, w2_ref[...]), g1_ref[...], be1_ref[...])
    x1b = _tie(x1, [_zero_of_last(raw_lo)]).astype(BF16)
    h = jax.nn.silu(_dot(x1b, wg_ref[...])) * _dot(x1b, wu_ref[...])

    after_x1 = _zero_of_last(x1)
    for c in range(N_LANE_BLOCKS):
        lo = slice(c * LANES, (c + 1) * LANES)
        hi = slice(D_MODEL + c * LANES, D_MODEL + (c + 1) * LANES)
        a = raw_ref[1 - slot, :, lo] + b1_ref[:, lo]
        g = raw_ref[1 - slot, :, hi] + b1_ref[:, hi]
        _push_history(hist_ref, step - 1, FUSED_TILE_M, c,
                      _tie_all(a * jax.nn.sigmoid(g), after_x1), CONV_A_PAD)
    conv_zeros = []
    for r0 in range(0, FUSED_TILE_M, ROW_CHUNK):
        for c in range(N_LANE_BLOCKS):
            acc = _conv_chunk(wdw_ref, hist_ref, c, r0, width=CONV_A_WIDTH, pad=CONV_A_PAD)
            conv_ref[r0:r0 + ROW_CHUNK, c * LANES:(c + 1) * LANES] = acc
            conv_zeros.append(_zero_of_last(acc))

    y = _tie_all(conv_ref[...], _zero_of_last(h)) + bdw_ref[...]
    y = jax.nn.silu(_layer_norm(y, lng_ref[...], lnb_ref[...]))
    mid_ref[...] = y.astype(BF16)
    norm_zeros = [_zero_of_last(y[r:r + SUBLANES]) for r in range(0, FUSED_TILE_M, ROW_CHUNK)]

    d = _dot(h.astype(BF16), wd_ref[...])
    raw_hi = _dot(_tie(x_ref[...], [_zero_of_last(d)]).astype(BF16), w1_ref[:, half:])
    raw_ref[slot, :, half:] = raw_hi
    x2 = _layer_norm(ALPHA * x1 + d, g2_ref[...], be2_ref[...])
    x2b = _tie(x2, [_zero_of_last(raw_hi)] + conv_zeros).astype(BF16)
    gate = _tie(jax.nn.sigmoid(_dot(x2b, wpg_ref[...])), norm_zeros)
    o_ref[...] = x2 + gate * _embedding(p_ref[...], wpp_ref, gn_ref)


def _layer_a_call(name, layer, x, pstack, resident, casts=()):
    n_tiles = N_ROWS // FUSED_TILE_M
    new = lambda i: (jnp.minimum(i, n_tiles - 1), 0)
    old = lambda i: (jnp.maximum(i - PIPELINE_LAG, 0), 0)
    old_p = lambda i: (layer, jnp.maximum(i - PIPELINE_LAG, 0), 0)
    arrays, specs = _resident(resident)
    c_arrays, c_in, c_out, c_shapes, spbs = _cast_specs(casts, n_tiles)
    out, *cast = pl.pallas_call(
        _with_casts(_layer_a_kernel, 3 + len(arrays), spbs),
        out_shape=[jax.ShapeDtypeStruct((N_ROWS, D_MODEL), F32)] + c_shapes,
        grid=(n_tiles + PIPELINE_LAG,),
        in_specs=[pl.BlockSpec((FUSED_TILE_M, D_MODEL), new),
                  pl.BlockSpec((FUSED_TILE_M, D_MODEL), old),
                  pl.BlockSpec((None, FUSED_TILE_M, D_PLE), old_p)] + specs + c_in,
        out_specs=[pl.BlockSpec((FUSED_TILE_M, D_MODEL), old)] + c_out,
        scratch_shapes=[pltpu.VMEM((2, FUSED_TILE_M, 2 * D_MODEL), F32),
                        pltpu.VMEM((FUSED_TILE_M, D_MODEL), BF16),
                        *_conv_scratch(FUSED_TILE_M, CONV_A_PAD)],
        compiler_params=pltpu.CompilerParams(
            dimension_semantics=("arbitrary",),
            vmem_limit_bytes=VMEM_LIMIT_BYTES),
        name=name,
    )(x, x, pstack, *arrays, *c_arrays)
    return out, cast


def _conv_scratch(tile_m, pad):
    return (pltpu.VMEM((N_LANE_BLOCKS, pad + tile_m, LANES), F32),
            pltpu.VMEM((tile_m, D_MODEL), F32))


def _vec(a):
    return a.reshape(1, -1).astype(F32)


def kernel(x, p, a_w_pw1, a_b_pw1, a_w_dw, a_b_dw, a_ln_g, a_ln_b, a_w_pw2,
           b_w_in, b_b_in, b_ln_g, b_ln_b, b_w_s, b_b_s, b_w_out,
           c_w_in, c_w_conv, c_w_out, ln1_g, ln1_b, ln2_g, ln2_b,
           ffn_w_gate, ffn_w_up, ffn_w_down, ple_w_gate, ple_w_proj, ple_norm_g):
    assert x.shape == (BATCH, SEQ, D_MODEL) and p.shape == (DEPTH, BATCH, SEQ, D_PLE)
    xf = x.reshape(N_ROWS, D_MODEL)
    pstack = p.reshape(DEPTH, N_ROWS, D_PLE)

    def ffn_stacks(i):
        return [(ffn_w_gate, i), (ffn_w_up, i), (ffn_w_down, i), (ple_w_gate, i), (ple_w_proj, i)]

    def mixer_stacks(i):
        m, j = i % N_MIXERS, i // N_MIXERS
        return [[(a_w_pw1, j), (a_w_pw2, j)], [(b_w_in, j), (b_w_out, j)],
                [(c_w_in, j), (c_w_out, j)]][m]

    mixer_w = [w[j].astype(BF16) for w, j in mixer_stacks(0)]
    ffn_w = [w[j].astype(BF16) for w, j in ffn_stacks(0)]
    for i in range(DEPTH):
        m, j = i % N_MIXERS, i // N_MIXERS
        nxt = i + 1 < DEPTH
        post = [_vec(ln1_g[i]), _vec(ln1_b[i])]
        ffn = ffn_w[:3] + [_vec(ln2_g[i]), _vec(ln2_b[i])] + ffn_w[3:] + [_vec(ple_norm_g[i])]
        if m == 0:
            casts = mixer_stacks(i + 1) + ffn_stacks(i + 1) if nxt else []
            xf, cast = _layer_a_call(
                f"layer_a_{i}", i, xf, pstack,
                [mixer_w[0], _vec(a_b_pw1[j]), _layer_of(a_w_dw.astype(F32), j),
                 _vec(a_b_dw[j]), _vec(a_ln_g[j]), _vec(a_ln_b[j]), mixer_w[1]] + post + ffn,
                casts)
            mixer_w, ffn_w = cast[:2], cast[2:]
            continue
        if m == 1:
            bias = jnp.repeat(jnp.transpose(b_b_s[j]).astype(F32), SGU_GROUP, axis=1)
            xf, next_mixer_w = _row_call(
                _mixer_b_kernel, f"mixer_b_{i}", xf, None,
                [mixer_w[0], _vec(b_b_in[j]), _vec(b_ln_g[j]), _vec(b_ln_b[j]),
                 _layer_of(b_w_s.astype(F32), j), bias, mixer_w[1]] + post,
                (pltpu.VMEM((TILE_M, SGU_WIDTH), BF16),),
                mixer_stacks(i + 1) if nxt else [])
        else:
            xf, next_mixer_w = _row_call(
                _mixer_c_kernel, f"mixer_c_{i}", xf, None,
                [mixer_w[0], _layer_of(c_w_conv.astype(F32), j), mixer_w[1]] + post,
                _conv_scratch(TILE_M, CONV_C_PAD),
                mixer_stacks(i + 1) if nxt else [])
        xf, next_ffn_w = _row_call(_ffn_kernel, f"ffn_{i}", xf, (i, pstack), ffn, (),
                                   ffn_stacks(i + 1) if nxt else [])
        mixer_w, ffn_w = next_mixer_w, next_ffn_w
    return xf.reshape(BATCH, SEQ, D_MODEL)
```

```python
import functools

import jax
import jax.numpy as jnp
from jax import lax
from jax.experimental import pallas as pl
from jax.experimental.pallas import tpu as pltpu

D_MODEL = 1024
BATCH = 4
SEQ = 4096
DEPTH = 4
CHUNK = 64
N_MIXERS = 3
CONV_A_WIDTH = 31
CONV_C_WIDTH = 3
SGU_BLOCK = 128
SGU_HEADS = 8
SGU_WIDTH = 2 * D_MODEL
SGU_GROUP = SGU_WIDTH // SGU_HEADS
D_FF = -(-8 * D_MODEL // (3 * 256)) * 256
D_PLE = 256
ALPHA = (2 * DEPTH) ** 0.25
LN_EPS = 1e-5
N_ROWS = BATCH * SEQ

LANES = 128
SUBLANES = 8
N_LANE_BLOCKS = D_MODEL // LANES
TILE_M = 1024
FUSED_TILE_M = 256
PIPELINE_LAG = 2
ROW_CHUNK = 32
CONV_A_PAD = 32
CONV_C_PAD = 8
VMEM_LIMIT_BYTES = 56 * 1024 * 1024

BF16 = jnp.bfloat16
F32 = jnp.float32
U32 = jnp.uint32


def _layer_norm(x, g, b):
    mu = jnp.mean(x, axis=-1, keepdims=True)
    xc = x - mu
    var = jnp.mean(xc * xc, axis=-1, keepdims=True)
    return xc * lax.rsqrt(var + LN_EPS) * g + b


def _dot(a, w):
    return jnp.dot(a, w, preferred_element_type=F32)


def _embedding(p, wpp_ref, gn_ref):
    pe = _dot(p.astype(BF16), wpp_ref[...])
    return pe * lax.rsqrt(jnp.mean(pe * pe, axis=-1, keepdims=True) + LN_EPS) * gn_ref[...]


def _push_history(hist_ref, tile, tile_m, c, vals, pad):
    first = tile % (SEQ // tile_m) == 0
    tail = hist_ref[c, tile_m:tile_m + pad, :]
    hist_ref[c, 0:pad, :] = jnp.where(first, 0.0, tail)
    hist_ref[c, pad:pad + tile_m, :] = vals


def _conv_chunk(w_ref, hist_ref, c, r0, *, width, pad):
    base = pad - (width - 1)
    acc = None
    for k in range(width):
        win = hist_ref[c, pl.ds(r0 + base + k, ROW_CHUNK), :]
        term = win * w_ref[k:k + 1, c * LANES:(c + 1) * LANES]
        acc = term if acc is None else acc + term
    return acc


HALF_M = TILE_M // 2


def _mixer_b_half(rows, x_ref, win_ref, bin_ref, lng_ref, lnb_ref, ws_ref, bs_ref,
                  wout_ref, g1_ref, be1_ref, o_ref, gated_ref):
    x = x_ref[rows, :]
    xb = x.astype(BF16)
    zv = _dot(xb, win_ref[:, SGU_WIDTH:])
    yield
    v = jax.nn.gelu(zv + bin_ref[:, SGU_WIDTH:])
    v = _layer_norm(v, lng_ref[...], lnb_ref[...]).astype(BF16)
    yield
    zu = _dot(xb, win_ref[:, :SGU_WIDTH])
    yield
    u = jax.nn.gelu(zu + bin_ref[:, :SGU_WIDTH])
    row = lax.broadcasted_iota(jnp.int32, (SGU_BLOCK, SGU_BLOCK), 0) // CHUNK
    col = lax.broadcasted_iota(jnp.int32, (SGU_BLOCK, SGU_BLOCK), 1) // CHUNK
    mask = row >= col
    for hd in range(SGU_HEADS):
        w_m = jnp.where(mask, ws_ref[hd], 0.0).astype(BF16)
        cs = slice(hd * SGU_GROUP, (hd + 1) * SGU_GROUP)
        for r0 in range(0, HALF_M, SGU_BLOCK):
            f = _dot(w_m, v[r0:r0 + SGU_BLOCK, cs]) + bs_ref[:, cs]
            gated_ref[rows.start + r0:rows.start + r0 + SGU_BLOCK, cs] = (
                u[r0:r0 + SGU_BLOCK, cs] * f).astype(BF16)
    yield
    m = _dot(gated_ref[rows, :], wout_ref[...])
    yield
    o_ref[rows, :] = _layer_norm(ALPHA * x + m, g1_ref[...], be1_ref[...])
    yield


def _mixer_b_kernel(*refs):
    h0 = _mixer_b_half(slice(0, HALF_M), *refs)
    h1 = _mixer_b_half(slice(HALF_M, TILE_M), *refs)
    for half in (h0, h1, h0, h0, h1, h1, h0, h0, h1, h1, h0, h1):
        next(half)


def _mixer_c_kernel(x_ref, win_ref, wconv_ref, wout_ref, g1_ref, be1_ref,
                    o_ref, hist_ref, conv_ref):
    x = x_ref[...]
    xb = x.astype(BF16)
    y = _dot(xb, win_ref[:, D_MODEL:])
    prod = y[:, :D_MODEL] * y[:, D_MODEL:]
    for c in range(N_LANE_BLOCKS):
        _push_history(hist_ref, pl.program_id(0), TILE_M, c,
                      prod[:, c * LANES:(c + 1) * LANES], CONV_C_PAD)
    for r0 in range(0, TILE_M, ROW_CHUNK):
        for c in range(N_LANE_BLOCKS):
            conv_ref[r0:r0 + ROW_CHUNK, c * LANES:(c + 1) * LANES] = _conv_chunk(
                wconv_ref, hist_ref, c, r0, width=CONV_C_WIDTH, pad=CONV_C_PAD)
    bg = _dot(xb, win_ref[:, :D_MODEL])
    m = _dot((bg * conv_ref[...]).astype(BF16), wout_ref[...])
    o_ref[...] = _layer_norm(ALPHA * x + m, g1_ref[...], be1_ref[...])


def _ffn_half(rows, x_ref, p_ref, wg_ref, wu_ref, wd_ref, g2_ref, be2_ref, wpg_ref,
              wpp_ref, gn_ref, o_ref):
    x1 = x_ref[rows, :]
    x1b = x1.astype(BF16)
    h = jax.nn.silu(_dot(x1b, wg_ref[...])) * _dot(x1b, wu_ref[...])
    yield
    d = _dot(h.astype(BF16), wd_ref[...])
    yield
    x2 = _layer_norm(ALPHA * x1 + d, g2_ref[...], be2_ref[...])
    gate = jax.nn.sigmoid(_dot(x2.astype(BF16), wpg_ref[...]))
    o_ref[rows, :] = x2 + gate * _embedding(p_ref[rows, :], wpp_ref, gn_ref)
    yield


def _ffn_kernel(*refs):
    h0 = _ffn_half(slice(0, HALF_M), *refs)
    h1 = _ffn_half(slice(HALF_M, TILE_M), *refs)
    for half in (h0, h0, h1, h0, h1, h1):
        next(half)


def _layer_of(stack, layer):
    return (stack, layer)


def _resident(operands):
    arrays, specs = [], []
    for op in operands:
        if isinstance(op, tuple):
            stack, layer = op
            shape = (None,) + stack.shape[1:]
            index = functools.partial(lambda l, n, i: (l,) + (0,) * n, layer, stack.ndim - 1)
            arrays.append(stack)
        else:
            shape = op.shape
            index = functools.partial(lambda n, i: (0,) * n, op.ndim)
            arrays.append(op)
        specs.append(pl.BlockSpec(shape, index, pipeline_mode=pl.Buffered(1)))
    return arrays, specs


BF16_ROWS = 16


def _cast_specs(casts, n_steps):
    arrays, in_specs, out_specs, out_shapes, spbs = [], [], [], [], []
    for stack, layer in casts:
        k, n = stack.shape[1:]
        n_blocks = n_steps
        while k % (n_blocks * BF16_ROWS):
            n_blocks //= 2
        spb = n_steps // n_blocks
        block = functools.partial(lambda spb, nb, i: jnp.minimum(i // spb, nb - 1), spb, n_blocks)
        arrays.append(stack)
        in_specs.append(pl.BlockSpec(
            (None, k // n_blocks, n),
            functools.partial(lambda block, layer, i: (layer, block(i), 0), block, layer)))
        out_specs.append(pl.BlockSpec(
            (k // n_blocks, n), functools.partial(lambda block, i: (block(i), 0), block)))
        out_shapes.append(jax.ShapeDtypeStruct((k, n), BF16))
        spbs.append(spb)
    return arrays, in_specs, out_specs, out_shapes, spbs


def _with_casts(body, n_in, spbs):
    n = len(spbs)

    def kernel(*refs):
        ins, srcs = refs[:n_in], refs[n_in:n_in + n]
        out, dsts, scratch = refs[n_in + n], refs[n_in + n + 1:n_in + 2 * n + 1], refs[n_in + 2 * n + 1:]
        for spb in sorted(set(spbs)):
            @pl.when(pl.program_id(0) % spb == 0)
            def _():
                for src, dst, s in zip(srcs, dsts, spbs):
                    if s == spb:
                        dst[...] = src[...].astype(BF16)
        body(*ins, out, *scratch)

    return kernel


def _row_call(body, name, x, p, resident, scratch_shapes=(), casts=()):
    rows = [pl.BlockSpec((TILE_M, D_MODEL), lambda i: (i, 0))]
    args = [x]
    if p is not None:
        layer, pstack = p
        rows.append(pl.BlockSpec((None, TILE_M, D_PLE), lambda i: (layer, i, 0)))
        args.append(pstack)
    arrays, specs = _resident(resident)
    n_steps = N_ROWS // TILE_M
    c_arrays, c_in, c_out, c_shapes, spbs = _cast_specs(casts, n_steps)
    out, *cast = pl.pallas_call(
        _with_casts(body, len(args) + len(arrays), spbs),
        out_shape=[jax.ShapeDtypeStruct((N_ROWS, D_MODEL), F32)] + c_shapes,
        grid=(n_steps,),
        in_specs=rows + specs + c_in,
        out_specs=[pl.BlockSpec((TILE_M, D_MODEL), lambda i: (i, 0))] + c_out,
        scratch_shapes=list(scratch_shapes),
        compiler_params=pltpu.CompilerParams(
            dimension_semantics=("arbitrary",),
            vmem_limit_bytes=VMEM_LIMIT_BYTES),
        name=name,
    )(*args, *arrays, *c_arrays)
    return out, cast


def _zero_of_last(v):
    bits = lax.bitcast_convert_type(v[-SUBLANES:, -LANES:], U32)
    return lax.shift_right_logical(lax.shift_right_logical(bits, U32(16)), U32(16))


def _as_f32_zero(zeros):
    return lax.bitcast_convert_type(functools.reduce(jnp.bitwise_or, zeros), F32)


def _tie(a, zeros):
    head = a[:SUBLANES, :LANES] + _as_f32_zero(zeros)
    top = jnp.concatenate([head, a[:SUBLANES, LANES:]], axis=1)
    return jnp.concatenate([top, a[SUBLANES:]], axis=0)


def _tie_all(a, zero):
    reps = (a.shape[0] // SUBLANES, a.shape[1] // LANES)
    return a + jnp.tile(_as_f32_zero([zero]), reps)


def _layer_a_kernel(x_ref, xold_ref, p_ref, w1_ref, b1_ref, wdw_ref, bdw_ref, lng_ref,
                    lnb_ref, w2_ref, g1_ref, be1_ref, *refs):
    wg_ref, wu_ref, wd_ref, g2_ref, be2_ref, wpg_ref, wpp_ref, gn_ref = refs[:8]
    o_ref, raw_ref, mid_ref, hist_ref, conv_ref = refs[8:]
    step = pl.program_id(0)

    @pl.when(step == 0)
    def _():
        for s in (raw_ref, mid_ref, hist_ref, conv_ref):
            s[...] = jnp.zeros(s.shape, s.dtype)

    slot = step % 2
    mid = mid_ref[...]

    half = D_MODEL
    raw_lo = _dot(x_ref[...].astype(BF16), w1_ref[:, :half])
    raw_ref[slot, :, :half] = raw_lo

    x1 = _layer_norm(ALPHA * xold_ref[...] + _dot(mid, w2_ref[...]), g1_ref[...], be1_ref[...])
    x1b = _tie(x1, [_zero_of_last(raw_lo)]).astype(BF16)
    h = jax.nn.silu(_dot(x1b, wg_ref[...])) * _dot(x1b, wu_ref[...])

    after_x1 = _zero_of_last(x1)
    for c in range(N_LANE_BLOCKS):
        lo = slice(c * LANES, (c + 1) * LANES)
        hi = slice(D_MODEL + c * LANES, D_MODEL + (c + 1) * LANES)
        a = raw_ref[1 - slot, :, lo] + b1_ref[:, lo]
        g = raw_ref[1 - slot, :, hi] + b1_ref[:, hi]
        _push_history(hist_ref, step - 1, FUSED_TILE_M, c,
                      _tie_all(a * jax.nn.sigmoid(g), after_x1), CONV_A_PAD)
    conv_zeros = []
    for r0 in range(0, FUSED_TILE_M, ROW_CHUNK):
        for c in range(N_LANE_BLOCKS):
            acc = _conv_chunk(wdw_ref, hist_ref, c, r0, width=CONV_A_WIDTH, pad=CONV_A_PAD)
            conv_ref[r0:r0 + ROW_CHUNK, c * LANES:(c + 1) * LANES] = acc
            conv_zeros.append(_zero_of_last(acc))

    y = _tie_all(conv_ref[...], _zero_of_last(h)) + bdw_ref[...]
    y = jax.nn.silu(_layer_norm(y, lng_ref[...], lnb_ref[...]))
    mid_ref[...] = y.astype(BF16)
    norm_zeros = [_zero_of_last(y[r:r + SUBLANES]) for r in range(0, FUSED_TILE_M, ROW_CHUNK)]

    d = _dot(h.astype(BF16), wd_ref[...])
    raw_hi = _dot(_tie(x_ref[...], [_zero_of_last(d)]).astype(BF16), w1_ref[:, half:])
    raw_ref[slot, :, half:] = raw_hi
    x2 = _layer_norm(ALPHA * x1 + d, g2_ref[...], be2_ref[...])
    x2b = _tie(x2, [_zero_of_last(raw_hi)] + conv_zeros).astype(BF16)
    gate = _tie(jax.nn.sigmoid(_dot(x2b, wpg_ref[...])), norm_zeros)
    o_ref[...] = x2 + gate * _embedding(p_ref[...], wpp_ref, gn_ref)


def _layer_a_call(name, layer, x, pstack, resident, casts=()):
    n_tiles = N_ROWS // FUSED_TILE_M
    new = lambda i: (jnp.minimum(i, n_tiles - 1), 0)
    old = lambda i: (jnp.maximum(i - PIPELINE_LAG, 0), 0)
    old_p = lambda i: (layer, jnp.maximum(i - PIPELINE_LAG, 0), 0)
    arrays, specs = _resident(resident)
    c_arrays, c_in, c_out, c_shapes, spbs = _cast_specs(casts, n_tiles)
    out, *cast = pl.pallas_call(
        _with_casts(_layer_a_kernel, 3 + len(arrays), spbs),
        out_shape=[jax.ShapeDtypeStruct((N_ROWS, D_MODEL), F32)] + c_shapes,
        grid=(n_tiles + PIPELINE_LAG,),
        in_specs=[pl.BlockSpec((FUSED_TILE_M, D_MODEL), new),
                  pl.BlockSpec((FUSED_TILE_M, D_MODEL), old),
                  pl.BlockSpec((None, FUSED_TILE_M, D_PLE), old_p)] + specs + c_in,
        out_specs=[pl.BlockSpec((FUSED_TILE_M, D_MODEL), old)] + c_out,
        scratch_shapes=[pltpu.VMEM((2, FUSED_TILE_M, 2 * D_MODEL), F32),
                        pltpu.VMEM((FUSED_TILE_M, D_MODEL), BF16),
                        *_conv_scratch(FUSED_TILE_M, CONV_A_PAD)],
        compiler_params=pltpu.CompilerParams(
            dimension_semantics=("arbitrary",),
            vmem_limit_bytes=VMEM_LIMIT_BYTES),
        name=name,
    )(x, x, pstack, *arrays, *c_arrays)
    return out, cast


def _conv_scratch(tile_m, pad):
    return (pltpu.VMEM((N_LANE_BLOCKS, pad + tile_m, LANES), F32),
            pltpu.VMEM((tile_m, D_MODEL), F32))


def _vec(a):
    return a.reshape(1, -1).astype(F32)


def kernel(x, p, a_w_pw1, a_b_pw1, a_w_dw, a_b_dw, a_ln_g, a_ln_b, a_w_pw2,
           b_w_in, b_b_in, b_ln_g, b_ln_b, b_w_s, b_b_s, b_w_out,
           c_w_in, c_w_conv, c_w_out, ln1_g, ln1_b, ln2_g, ln2_b,
           ffn_w_gate, ffn_w_up, ffn_w_down, ple_w_gate, ple_w_proj, ple_norm_g):
    assert x.shape == (BATCH, SEQ, D_MODEL) and p.shape == (DEPTH, BATCH, SEQ, D_PLE)
    xf = x.reshape(N_ROWS, D_MODEL)
    pstack = p.reshape(DEPTH, N_ROWS, D_PLE)

    def ffn_stacks(i):
        return [(ffn_w_gate, i), (ffn_w_up, i), (ffn_w_down, i), (ple_w_gate, i), (ple_w_proj, i)]

    def mixer_stacks(i):
        m, j = i % N_MIXERS, i // N_MIXERS
        return [[(a_w_pw1, j), (a_w_pw2, j)], [(b_w_in, j), (b_w_out, j)],
                [(c_w_in, j), (c_w_out, j)]][m]

    mixer_w = [w[j].astype(BF16) for w, j in mixer_stacks(0)]
    ffn_w = [w[j].astype(BF16) for w, j in ffn_stacks(0)]
    for i in range(DEPTH):
        m, j = i % N_MIXERS, i // N_MIXERS
        nxt = i + 1 < DEPTH
        post = [_vec(ln1_g[i]), _vec(ln1_b[i])]
        ffn = ffn_w[:3] + [_vec(ln2_g[i]), _vec(ln2_b[i])] + ffn_w[3:] + [_vec(ple_norm_g[i])]
        if m == 0:
            casts = mixer_stacks(i + 1) + ffn_stacks(i + 1) if nxt else []
            xf, cast = _layer_a_call(
                f"layer_a_{i}", i, xf, pstack,
                [mixer_w[0], _vec(a_b_pw1[j]), _layer_of(a_w_dw.astype(F32), j),
                 _vec(a_b_dw[j]), _vec(a_ln_g[j]), _vec(a_ln_b[j]), mixer_w[1]] + post + ffn,
                casts)
            mixer_w, ffn_w = cast[:2], cast[2:]
            continue
        if m == 1:
            bias = jnp.repeat(jnp.transpose(b_b_s[j]).astype(F32), SGU_GROUP, axis=1)
            xf, next_mixer_w = _row_call(
                _mixer_b_kernel, f"mixer_b_{i}", xf, None,
                [mixer_w[0], _vec(b_b_in[j]), _vec(b_ln_g[j]), _vec(b_ln_b[j]),
                 _layer_of(b_w_s.astype(F32), j), bias, mixer_w[1]] + post,
                (pltpu.VMEM((TILE_M, SGU_WIDTH), BF16),),
                mixer_stacks(i + 1) if nxt else [])
        else:
            xf, next_mixer_w = _row_call(
                _mixer_c_kernel, f"mixer_c_{i}", xf, None,
                [mixer_w[0], _layer_of(c_w_conv.astype(F32), j), mixer_w[1]] + post,
                _conv_scratch(TILE_M, CONV_C_PAD),
                mixer_stacks(i + 1) if nxt else [])
        xf, next_ffn_w = _row_call(_ffn_kernel, f"ffn_{i}", xf, (i, pstack), ffn, (),
                                   ffn_stacks(i + 1) if nxt else [])
        mixer_w, ffn_w = next_mixer_w, next_ffn_w
    return xf.reshape(BATCH, SEQ, D_MODEL)
```

```python
import functools

import jax
import jax.numpy as jnp
from jax import lax
from jax.experimental import pallas as pl
from jax.experimental.pallas import tpu as pltpu

D_MODEL = 1024
BATCH = 4
SEQ = 4096
DEPTH = 4
CHUNK = 64
N_MIXERS = 3
CONV_A_WIDTH = 31
CONV_C_WIDTH = 3
SGU_BLOCK = 128
SGU_HEADS = 8
SGU_WIDTH = 2 * D_MODEL
SGU_GROUP = SGU_WIDTH // SGU_HEADS
D_FF = -(-8 * D_MODEL // (3 * 256)) * 256
D_PLE = 256
ALPHA = (2 * DEPTH) ** 0.25
LN_EPS = 1e-5
N_ROWS = BATCH * SEQ

LANES = 128
SUBLANES = 8
N_LANE_BLOCKS = D_MODEL // LANES
TILE_M = 1024
FUSED_TILE_M = 256
PIPELINE_LAG = 2
ROW_CHUNK = 32
CONV_A_PAD = 32
CONV_C_PAD = 8
VMEM_LIMIT_BYTES = 56 * 1024 * 1024

BF16 = jnp.bfloat16
F32 = jnp.float32
U32 = jnp.uint32


def _layer_norm(x, g, b):
    mu = jnp.mean(x, axis=-1, keepdims=True)
    xc = x - mu
    var = jnp.mean(xc * xc, axis=-1, keepdims=True)
    return xc * lax.rsqrt(var + LN_EPS) * g + b


def _dot(a, w):
    return jnp.dot(a, w, preferred_element_type=F32)


def _embedding(p, wpp_ref, gn_ref):
    pe = _dot(p.astype(BF16), wpp_ref[...])
    return pe * lax.rsqrt(jnp.mean(pe * pe, axis=-1, keepdims=True) + LN_EPS) * gn_ref[...]


def _push_history(hist_ref, tile, tile_m, c, vals, pad):
    first = tile % (SEQ // tile_m) == 0
    tail = hist_ref[c, tile_m:tile_m + pad, :]
    hist_ref[c, 0:pad, :] = jnp.where(first, 0.0, tail)
    hist_ref[c, pad:pad + tile_m, :] = vals


def _conv_chunk(w_ref, hist_ref, c, r0, *, width, pad):
    base = pad - (width - 1)
    acc = None
    for k in range(width):
        win = hist_ref[c, pl.ds(r0 + base + k, ROW_CHUNK), :]
        term = win * w_ref[k:k + 1, c * LANES:(c + 1) * LANES]
        acc = term if acc is None else acc + term
    return acc


HALF_M = TILE_M // 2


def _mixer_b_half(rows, x_ref, win_ref, bin_ref, lng_ref, lnb_ref, ws_ref, bs_ref,
                  wout_ref, g1_ref, be1_ref, o_ref, gated_ref):
    x = x_ref[rows, :]
    xb = x.astype(BF16)
    zv = _dot(xb, win_ref[:, SGU_WIDTH:])
    yield
    v = jax.nn.gelu(zv + bin_ref[:, SGU_WIDTH:])
    v = _layer_norm(v, lng_ref[...], lnb_ref[...]).astype(BF16)
    yield
    zu = _dot(xb, win_ref[:, :SGU_WIDTH])
    yield
    u = jax.nn.gelu(zu + bin_ref[:, :SGU_WIDTH])
    row = lax.broadcasted_iota(jnp.int32, (SGU_BLOCK, SGU_BLOCK), 0) // CHUNK
    col = lax.broadcasted_iota(jnp.int32, (SGU_BLOCK, SGU_BLOCK), 1) // CHUNK
    mask = row >= col
    for hd in range(SGU_HEADS):
        w_m = jnp.where(mask, ws_ref[hd], 0.0).astype(BF16)
        cs = slice(hd * SGU_GROUP, (hd + 1) * SGU_GROUP)
        for r0 in range(0, HALF_M, SGU_BLOCK):
            f = _dot(w_m, v[r0:r0 + SGU_BLOCK, cs]) + bs_ref[:, cs]
            gated_ref[rows.start + r0:rows.start + r0 + SGU_BLOCK, cs] = (
                u[r0:r0 + SGU_BLOCK, cs] * f).astype(BF16)
    yield
    m = _dot(gated_ref[rows, :], wout_ref[...])
    yield
    o_ref[rows, :] = _layer_norm(ALPHA * x + m, g1_ref[...], be1_ref[...])
    yield


def _mixer_b_kernel(*refs):
    h0 = _mixer_b_half(slice(0, HALF_M), *refs)
    h1 = _mixer_b_half(slice(HALF_M, TILE_M), *refs)
    for half in (h0, h1, h0, h0, h1, h1, h0, h0, h1, h1, h0, h1):
        next(half)


def _mixer_c_kernel(x_ref, win_ref, wconv_ref, wout_ref, g1_ref, be1_ref,
                    o_ref, hist_ref, conv_ref):
    x = x_ref[...]
    xb = x.astype(BF16)
    y = _dot(xb, win_ref[:, D_MODEL:])
    prod = y[:, :D_MODEL] * y[:, D_MODEL:]
    for c in range(N_LANE_BLOCKS):
        _push_history(hist_ref, pl.program_id(0), TILE_M, c,
                      prod[:, c * LANES:(c + 1) * LANES], CONV_C_PAD)
    for r0 in range(0, TILE_M, ROW_CHUNK):
        for c in range(N_LANE_BLOCKS):
            conv_ref[r0:r0 + ROW_CHUNK, c * LANES:(c + 1) * LANES] = _conv_chunk(
                wconv_ref, hist_ref, c, r0, width=CONV_C_WIDTH, pad=CONV_C_PAD)
    bg = _dot(xb, win_ref[:, :D_MODEL])
    m = _dot((bg * conv_ref[...]).astype(BF16), wout_ref[...])
    o_ref[...] = _layer_norm(ALPHA * x + m, g1_ref[...], be1_ref[...])


def _ffn_half(rows, x_ref, p_ref, wg_ref, wu_ref, wd_ref, g2_ref, be2_ref, wpg_ref,
              wpp_ref, gn_ref, o_ref):
    x1 = x_ref[rows, :]
    x1b = x1.astype(BF16)
    h = jax.nn.silu(_dot(x1b, wg_ref[...])) * _dot(x1b, wu_ref[...])
    yield
    d = _dot(h.astype(BF16), wd_ref[...])
    yield
    x2 = _layer_norm(ALPHA * x1 + d, g2_ref[...], be2_ref[...])
    gate = jax.nn.sigmoid(_dot(x2.astype(BF16), wpg_ref[...]))
    o_ref[rows, :] = x2 + gate * _embedding(p_ref[rows, :], wpp_ref, gn_ref)
    yield


def _ffn_kernel(*refs):
    h0 = _ffn_half(slice(0, HALF_M), *refs)
    h1 = _ffn_half(slice(HALF_M, TILE_M), *refs)
    for half in (h0, h0, h1, h0, h1, h1):
        next(half)


def _layer_of(stack, layer):
    return (stack, layer)


def _resident(operands):
    arrays, specs = [], []
    for op in operands:
        if isinstance(op, tuple):
            stack, layer = op
            shape = (None,) + stack.shape[1:]
            index = functools.partial(lambda l, n, i: (l,) + (0,) * n, layer, stack.ndim - 1)
            arrays.append(stack)
        else:
            shape = op.shape
            index = functools.partial(lambda n, i: (0,) * n, op.ndim)
            arrays.append(op)
        specs.append(pl.BlockSpec(shape, index, pipeline_mode=pl.Buffered(1)))
    return arrays, specs


BF16_ROWS = 16


def _cast_specs(casts, n_steps):
    arrays, in_specs, out_specs, out_shapes, spbs = [], [], [], [], []
    for stack, layer in casts:
        k, n = stack.shape[1:]
        n_blocks = n_steps
        while k % (n_blocks * BF16_ROWS):
            n_blocks //= 2
        spb = n_steps // n_blocks
        block = functools.partial(lambda spb, nb, i: jnp.minimum(i // spb, nb - 1), spb, n_blocks)
        arrays.append(stack)
        in_specs.append(pl.BlockSpec(
            (None, k // n_blocks, n),
            functools.partial(lambda block, layer, i: (layer, block(i), 0), block, layer)))
        out_specs.append(pl.BlockSpec(
            (k // n_blocks, n), functools.partial(lambda block, i: (block(i), 0), block)))
        out_shapes.append(jax.ShapeDtypeStruct((k, n), BF16))
        spbs.append(spb)
    return arrays, in_specs, out_specs, out_shapes, spbs


def _with_casts(body, n_in, spbs):
    n = len(spbs)

    def kernel(*refs):
        ins, srcs = refs[:n_in], refs[n_in:n_in + n]
        out, dsts, scratch = refs[n_in + n], refs[n_in + n + 1:n_in + 2 * n + 1], refs[n_in + 2 * n + 1:]
        for spb in sorted(set(spbs)):
            @pl.when(pl.program_id(0) % spb == 0)
            def _():
                for src, dst, s in zip(srcs, dsts, spbs):
                    if s == spb:
                        dst[...] = src[...].astype(BF16)
        body(*ins, out, *scratch)

    return kernel


def _row_call(body, name, x, p, resident, scratch_shapes=(), casts=()):
    rows = [pl.BlockSpec((TILE_M, D_MODEL), lambda i: (i, 0))]
    args = [x]
    if p is not None:
        layer, pstack = p
        rows.append(pl.BlockSpec((None, TILE_M, D_PLE), lambda i: (layer, i, 0)))
        args.append(pstack)
    arrays, specs = _resident(resident)
    n_steps = N_ROWS // TILE_M
    c_arrays, c_in, c_out, c_shapes, spbs = _cast_specs(casts, n_steps)
    out, *cast = pl.pallas_call(
        _with_casts(body, len(args) + len(arrays), spbs),
        out_shape=[jax.ShapeDtypeStruct((N_ROWS, D_MODEL), F32)] + c_shapes,
        grid=(n_steps,),
        in_specs=rows + specs + c_in,
        out_specs=[pl.BlockSpec((TILE_M, D_MODEL), lambda i: (i, 0))] + c_out,
        scratch_shapes=list(scratch_shapes),
        compiler_params=pltpu.CompilerParams(
            dimension_semantics=("arbitrary",),
            vmem_limit_bytes=VMEM_LIMIT_BYTES),
        name=name,
    )(*args, *arrays, *c_arrays)
    return out, cast


def _zero_of_last(v):
    bits = lax.bitcast_convert_type(v[-SUBLANES:, -LANES:], U32)
    return lax.shift_right_logical(lax.shift_right_logical(bits, U32(16)), U32(16))


def _as_f32_zero(zeros):
    return lax.bitcast_convert_type(functools.reduce(jnp.bitwise_or, zeros), F32)


def _tie(a, zeros):
    head = a[:SUBLANES, :LANES] + _as_f32_zero(zeros)
    top = jnp.concatenate([head, a[:SUBLANES, LANES:]], axis=1)
    return jnp.concatenate([top, a[SUBLANES:]], axis=0)


def _tie_all(a, zero):
    reps = (a.shape[0] // SUBLANES, a.shape[1] // LANES)
    return a + jnp.tile(_as_f32_zero([zero]), reps)


def _layer_a_kernel(x_ref, xold_ref, p_ref, w1_ref, b1_ref, wdw_ref, bdw_ref, lng_ref,
                    lnb_ref, w2_ref, g1_ref, be1_ref, *refs):
    wg_ref, wu_ref, wd_ref, g2_ref, be2_ref, wpg_ref, wpp_ref, gn_ref = refs[:8]
    o_ref, raw_ref, mid_ref, hist_ref, conv_ref = refs[8:]
    step = pl.program_id(0)

    @pl.when(step == 0)
    def _():
        for s in (raw_ref, mid_ref, hist_ref, conv_ref):
            s[...] = jnp.zeros(s.shape, s.dtype)

    slot = step % 2
    mid = mid_ref[...]

    half = D_MODEL
    raw_lo = _dot(x_ref[...].astype(BF16), w1_ref[:, :half])
    raw_ref[slot, :, :half] = raw_lo

    x1 = _layer_norm(ALPHA * xold_ref[...] + _dot(mid, w2_ref[...]), g1_ref[...], be1_ref[...])
    x1b = _tie(x1, [_zero_of_last(raw_lo)]).astype(BF16)
    h = jax.nn.silu(_dot(x1b, wg_ref[...])) * _dot(x1b, wu_ref[...])

    after_x1 = _zero_of_last(x1)
    for c in range(N_LANE_BLOCKS):
        lo = slice(c * LANES, (c + 1) * LANES)
        hi = slice(D_MODEL + c * LANES, D_MODEL + (c + 1) * LANES)
        a = raw_ref[1 - slot, :, lo] + b1_ref[:, lo]
        g = raw_ref[1 - slot, :, hi] + b1_ref[:, hi]
        _push_history(hist_ref, step - 1, FUSED_TILE_M, c,
                      _tie_all(a * jax.nn.sigmoid(g), after_x1), CONV_A_PAD)
    conv_zeros = []
    for r0 in range(0, FUSED_TILE_M, ROW_CHUNK):
        for c in range(N_LANE_BLOCKS):
            acc = _conv_chunk(wdw_ref, hist_ref, c, r0, width=CONV_A_WIDTH, pad=CONV_A_PAD)
            conv_ref[r0:r0 + ROW_CHUNK, c * LANES:(c + 1) * LANES] = acc
            conv_zeros.append(_zero_of_last(acc))

    y = _tie_all(conv_ref[...], _zero_of_last(h)) + bdw_ref[...]
    y = jax.nn.silu(_layer_norm(y, lng_ref[...], lnb_ref[...]))
    mid_ref[...] = y.astype(BF16)
    norm_zeros = [_zero_of_last(y[r:r + SUBLANES]) for r in range(0, FUSED_TILE_M, ROW_CHUNK)]

    d = _dot(h.astype(BF16), wd_ref[...])
    raw_hi = _dot(_tie(x_ref[...], [_zero_of_last(d)]).astype(BF16), w1_ref[:, half:])
    raw_ref[slot, :, half:] = raw_hi
    x2 = _layer_norm(ALPHA * x1 + d, g2_ref[...], be2_ref[...])
    x2b = _tie(x2, [_zero_of_last(raw_hi)] + conv_zeros).astype(BF16)
    gate = _tie(jax.nn.sigmoid(_dot(x2b, wpg_ref[...])), norm_zeros)
    o_ref[...] = x2 + gate * _embedding(p_ref[...], wpp_ref, gn_ref)


def _layer_a_call(name, layer, x, pstack, resident, casts=()):
    n_tiles = N_ROWS // FUSED_TILE_M
    new = lambda i: (jnp.minimum(i, n_tiles - 1), 0)
    old = lambda i: (jnp.maximum(i - PIPELINE_LAG, 0), 0)
    old_p = lambda i: (layer, jnp.maximum(i - PIPELINE_LAG, 0), 0)
    arrays, specs = _resident(resident)
    c_arrays, c_in, c_out, c_shapes, spbs = _cast_specs(casts, n_tiles)
    out, *cast = pl.pallas_call(
        _with_casts(_layer_a_kernel, 3 + len(arrays), spbs),
        out_shape=[jax.ShapeDtypeStruct((N_ROWS, D_MODEL), F32)] + c_shapes,
        grid=(n_tiles + PIPELINE_LAG,),
        in_specs=[pl.BlockSpec((FUSED_TILE_M, D_MODEL), new),
                  pl.BlockSpec((FUSED_TILE_M, D_MODEL), old),
                  pl.BlockSpec((None, FUSED_TILE_M, D_PLE), old_p)] + specs + c_in,
        out_specs=[pl.BlockSpec((FUSED_TILE_M, D_MODEL), old)] + c_out,
        scratch_shapes=[pltpu.VMEM((2, FUSED_TILE_M, 2 * D_MODEL), F32),
                        pltpu.VMEM((FUSED_TILE_M, D_MODEL), BF16),
                        *_conv_scratch(FUSED_TILE_M, CONV_A_PAD)],
        compiler_params=pltpu.CompilerParams(
            dimension_semantics=("arbitrary",),
            vmem_limit_bytes=VMEM_LIMIT_BYTES),
        name=name,
    )(x, x, pstack, *arrays, *c_arrays)
    return out, cast


SLAB_SKEW_ROWS = SUBLANES


def _conv_scratch(tile_m, pad):
    return (pltpu.VMEM((N_LANE_BLOCKS, pad + tile_m + SLAB_SKEW_ROWS, LANES), F32),
            pltpu.VMEM((tile_m, D_MODEL), F32))


def _vec(a):
    return a.reshape(1, -1).astype(F32)


def kernel(x, p, a_w_pw1, a_b_pw1, a_w_dw, a_b_dw, a_ln_g, a_ln_b, a_w_pw2,
           b_w_in, b_b_in, b_ln_g, b_ln_b, b_w_s, b_b_s, b_w_out,
           c_w_in, c_w_conv, c_w_out, ln1_g, ln1_b, ln2_g, ln2_b,
           ffn_w_gate, ffn_w_up, ffn_w_down, ple_w_gate, ple_w_proj, ple_norm_g):
    assert x.shape == (BATCH, SEQ, D_MODEL) and p.shape == (DEPTH, BATCH, SEQ, D_PLE)
    xf = x.reshape(N_ROWS, D_MODEL)
    pstack = p.reshape(DEPTH, N_ROWS, D_PLE)

    def ffn_stacks(i):
        return [(ffn_w_gate, i), (ffn_w_up, i), (ffn_w_down, i), (ple_w_gate, i), (ple_w_proj, i)]

    def mixer_stacks(i):
        m, j = i % N_MIXERS, i // N_MIXERS
        return [[(a_w_pw1, j), (a_w_pw2, j)], [(b_w_in, j), (b_w_out, j)],
                [(c_w_in, j), (c_w_out, j)]][m]

    mixer_w = [w[j].astype(BF16) for w, j in mixer_stacks(0)]
    ffn_w = [w[j].astype(BF16) for w, j in ffn_stacks(0)]
    for i in range(DEPTH):
        m, j = i % N_MIXERS, i // N_MIXERS
        nxt = i + 1 < DEPTH
        post = [_vec(ln1_g[i]), _vec(ln1_b[i])]
        ffn = ffn_w[:3] + [_vec(ln2_g[i]), _vec(ln2_b[i])] + ffn_w[3:] + [_vec(ple_norm_g[i])]
        if m == 0:
            casts = mixer_stacks(i + 1) + ffn_stacks(i + 1) if nxt else []
            xf, cast = _layer_a_call(
                f"layer_a_{i}", i, xf, pstack,
                [mixer_w[0], _vec(a_b_pw1[j]), _layer_of(a_w_dw.astype(F32), j),
                 _vec(a_b_dw[j]), _vec(a_ln_g[j]), _vec(a_ln_b[j]), mixer_w[1]] + post + ffn,
                casts)
            mixer_w, ffn_w = cast[:2], cast[2:]
            continue
        if m == 1:
            bias = jnp.repeat(jnp.transpose(b_b_s[j]).astype(F32), SGU_GROUP, axis=1)
            xf, next_mixer_w = _row_call(
                _mixer_b_kernel, f"mixer_b_{i}", xf, None,
                [mixer_w[0], _vec(b_b_in[j]), _vec(b_ln_g[j]), _vec(b_ln_b[j]),
                 _layer_of(b_w_s.astype(F32), j), bias, mixer_w[1]] + post,
                (pltpu.VMEM((TILE_M, SGU_WIDTH), BF16),),
                mixer_stacks(i + 1) if nxt else [])
        else:
            xf, next_mixer_w = _row_call(
                _mixer_c_kernel, f"mixer_c_{i}", xf, None,
                [mixer_w[0], _layer_of(c_w_conv.astype(F32), j), mixer_w[1]] + post,
                _conv_scratch(TILE_M, CONV_C_PAD),
                mixer_stacks(i + 1) if nxt else [])
        xf, next_ffn_w = _row_call(_ffn_kernel, f"ffn_{i}", xf, (i, pstack), ffn, (),
                                   ffn_stacks(i + 1) if nxt else [])
        mixer_w, ffn_w = next_mixer_w, next_ffn_w
    return xf.reshape(BATCH, SEQ, D_MODEL)
```
